```python
import jax, jax.numpy as jnp
from jax import lax
import numpy as np

D_MODEL = 1024
BATCH = 8
SEQ = 2048
DEPTH = 4
DEC_BATCH = 32
DEC_SEQ = 8
PAST_LEN = 8192
PAGE_SIZE = 128

N_A = DEPTH // 2
N_B = DEPTH - N_A
H_RET = 6
DK_RET = 64
DV_RET = 128
H_FOX = 12
DH = 64
H_MEM = 4
N_MEM = 256
D_RET_QK = H_RET * DK_RET
D_RET_V = H_RET * DV_RET
D_FOX = H_FOX * DH
D_MEMQ = H_MEM * DH
D_IN_A = 2 * D_RET_QK + 2 * D_RET_V + D_MEMQ
D_IN_B = D_FOX + D_MEMQ
D_CAT_A = D_RET_V + D_MEMQ
D_CAT_B = D_FOX + D_MEMQ
D_KV_SH = 2 * D_FOX + H_FOX
D_FF = 2816
CONV_W = 3
RET_CHUNK = 128
Q_BLOCK = 128
ROPE_BASE = 10000.0
EPS = 1e-6

kernel_name = 'yoco_retention_fox_convffn_memory_step'


def rmsnorm(x, g):
    xf = x.astype(jnp.float32)
    y = xf * lax.rsqrt(jnp.mean(xf * xf, axis=-1, keepdims=True) + EPS)
    return (y * g.astype(jnp.float32)).astype(x.dtype)


def head_norm(o):
    c = o - jnp.mean(o, axis=-1, keepdims=True)
    return c * lax.rsqrt(jnp.mean(c * c, axis=-1, keepdims=True) + EPS)


def rotary(x, pos):
    half = x.shape[-1] // 2
    inv = ROPE_BASE ** (-jnp.arange(half, dtype=jnp.float32) / half)
    ang = pos.astype(jnp.float32)[:, None] * inv[None, :]
    cos = jnp.cos(ang)[None, :, None, :]
    sin = jnp.sin(ang)[None, :, None, :]
    xf = x.astype(jnp.float32)
    x1, x2 = xf[..., :half], xf[..., half:]
    return jnp.concatenate([x1 * cos - x2 * sin, x1 * sin + x2 * cos], axis=-1).astype(x.dtype)


def retention(q, k, v, S0):
    B, L, H, _ = q.shape
    Dv = v.shape[-1]
    C = RET_CHUNK if L % RET_CHUNK == 0 else L
    n = L // C
    log_gamma = jnp.log1p(-(2.0 ** (-5.0 - jnp.arange(H, dtype=jnp.float32))))
    idx = jnp.arange(C, dtype=jnp.float32)
    rel = idx[:, None] - idx[None, :]
    dmask = jnp.where(rel >= 0, jnp.exp(log_gamma[:, None, None] * jnp.maximum(rel, 0.0)), 0.0)
    qdec = jnp.exp(log_gamma[None, :] * (idx[:, None] + 1.0))
    kdec = jnp.exp(log_gamma[None, :] * (C - 1.0 - idx[:, None]))
    sdec = jnp.exp(log_gamma * C)

    def to_chunks(t):
        return t.astype(jnp.float32).reshape(B, n, C, H, t.shape[-1]).transpose(1, 0, 2, 3, 4)

    def step(S, qkv):
        qc, kc, vc = qkv
        inner = jnp.einsum('bihd,bjhd->bhij', qc, kc) * dmask[None]
        o = (jnp.einsum('bhij,bjhe->bihe', inner, vc)
             + jnp.einsum('bihd,bhde->bihe', qc * qdec[None, :, :, None], S))
        S = sdec[None, :, None, None] * S + jnp.einsum('bjhd,bjhe->bhde', kc * kdec[None, :, :, None], vc)
        return S, o

    S, o = lax.scan(step, S0.astype(jnp.float32), (to_chunks(q), to_chunks(k), to_chunks(v)))
    o = o.transpose(1, 0, 2, 3, 4).reshape(B, L, H, Dv)
    return o, S


def mem_attend(qm, mk, mv):
    B, L, _ = qm.shape
    q = qm.reshape(B, L, H_MEM, DH)
    s = jnp.einsum('blhd,bmhd->bhlm', q, mk.astype(q.dtype)).astype(jnp.float32) * (DH ** -0.5)
    p = jax.nn.softmax(s, axis=-1).astype(q.dtype)
    return jnp.einsum('bhlm,bmhd->blhd', p, mv.astype(q.dtype)).reshape(B, L, D_MEMQ)


def fox_attention(q, k, v, Fq, Fk, q_pos, k_pos):
    B, Lq, H, Dh = q.shape
    FkT = jnp.transpose(Fk.astype(jnp.float32), (0, 2, 1))

    def block(args):
        qb, Fqb, qpb = args
        s = jnp.einsum('bqhd,bkhd->bhqk', qb, k).astype(jnp.float32) * (Dh ** -0.5)
        s = s + jnp.transpose(Fqb.astype(jnp.float32), (0, 2, 1))[..., :, None] - FkT[:, :, None, :]
        s = jnp.where((k_pos[None, :] <= qpb[:, None])[None, None], s, -jnp.inf)
        p = jax.nn.softmax(s, axis=-1).astype(v.dtype)
        return jnp.einsum('bhqk,bkhd->bqhd', p, v)

    Qb = Q_BLOCK if Lq % Q_BLOCK == 0 else Lq
    nb = Lq // Qb
    qs = q.reshape(B, nb, Qb, H, Dh).transpose(1, 0, 2, 3, 4)
    Fqs = Fq.reshape(B, nb, Qb, H).transpose(1, 0, 2, 3)
    qps = q_pos.reshape(nb, Qb)
    out = lax.map(block, (qs, Fqs, qps))
    return out.transpose(1, 0, 2, 3, 4).reshape(B, Lq, H, Dh)


def retention_mixer(a, w_in, w_out, mk, mv, S0, pos):
    B, L, _ = a.shape
    z = a @ w_in
    o1 = D_RET_QK
    o2 = 2 * D_RET_QK
    o3 = o2 + D_RET_V
    o4 = o3 + D_RET_V
    q = rotary(z[..., :o1].reshape(B, L, H_RET, DK_RET), pos)
    k = rotary(z[..., o1:o2].reshape(B, L, H_RET, DK_RET), pos) * (DK_RET ** -0.5)
    v = z[..., o2:o3].reshape(B, L, H_RET, DV_RET)
    g = z[..., o3:o4]
    r, S = retention(q, k, v, S0)
    r = (jax.nn.silu(g.astype(jnp.float32)) * head_norm(r).reshape(B, L, D_RET_V)).astype(a.dtype)
    m = mem_attend(z[..., o4:], mk, mv)
    return jnp.concatenate([r, m], axis=-1) @ w_out, S


def fox_mixer(a, w_in, w_out, keys, vals, Fq, Fk, q_pos, k_pos, mk, mv):
    B, L, _ = a.shape
    z = a @ w_in
    qf = z[..., :D_FOX].reshape(B, L, H_FOX, DH)
    o = fox_attention(qf, keys.astype(qf.dtype), vals.astype(qf.dtype), Fq, Fk, q_pos, k_pos).reshape(B, L, D_FOX)
    m = mem_attend(z[..., D_FOX:], mk, mv)
    return jnp.concatenate([o, m], axis=-1) @ w_out


def shared_kv(h, g_kv, w_kv, b_f):
    B, L, _ = h.shape
    z = rmsnorm(h, g_kv) @ w_kv
    k = z[..., :D_FOX].reshape(B, L, H_FOX, DH)
    v = z[..., D_FOX:2 * D_FOX].reshape(B, L, H_FOX, DH)
    logf = jax.nn.log_sigmoid((z[..., 2 * D_FOX:] + b_f).astype(jnp.float32))
    return k, v, logf


def conv_ffn(a, w_up, cw, cb, w_down, buf):
    L = a.shape[1]
    u = a @ w_up
    ext = jnp.concatenate([buf.astype(u.dtype), u], axis=1)
    c = cb
    for i in range(CONV_W):
        c = c + cw[i] * ext[:, i:i + L]
    gate, val = c[..., :D_FF], c[..., D_FF:]
    y = (jax.nn.gelu(gate, approximate=True) * val) @ w_down
    return y, ext[:, L:]


def trunk(x, pos, mem_k, mem_v, ret_state, conv_state, past_k, past_v, past_logf, p):
    L = x.shape[1]
    h = x
    new_ret, new_conv = [], []
    k_sh = v_sh = logf_sh = None
    keys = vals = F = Fq = k_pos = None
    for layer in range(DEPTH):
        a = rmsnorm(h, p['g_mix_pre'][layer])
        if layer < N_A:
            o, S = retention_mixer(a, p['w_in_a'][layer], p['w_out_a'][layer], mem_k[layer], mem_v[layer],
                                   ret_state[layer], pos)
            new_ret.append(S.astype(x.dtype))
        else:
            if layer == N_A:
                k_sh, v_sh, logf_sh = shared_kv(h, p['g_kv'], p['w_kv_shared'], p['b_f'])
                if past_k is None:
                    keys, vals, logf_all, k_pos = k_sh, v_sh, logf_sh, pos
                else:
                    past_len = past_k.shape[1]
                    keys = jnp.concatenate([past_k.astype(k_sh.dtype), k_sh], axis=1)
                    vals = jnp.concatenate([past_v.astype(v_sh.dtype), v_sh], axis=1)
                    logf_all = jnp.concatenate([past_logf.astype(jnp.float32), logf_sh], axis=1)
                    k_pos = jnp.arange(past_len + L, dtype=jnp.int32)
                F = jnp.cumsum(logf_all, axis=1)
                F = F - F[:, -1:]
                Fq = F[:, -L:]
            j = layer - N_A
            o = fox_mixer(a, p['w_in_b'][j], p['w_out_b'][j], keys, vals, Fq, F, pos, k_pos,
                          mem_k[layer], mem_v[layer])
        h = h + rmsnorm(o, p['g_mix_post'][layer])
        f, buf = conv_ffn(rmsnorm(h, p['g_ffn_pre'][layer]), p['w_ffn_up'][layer], p['conv_w'][layer],
                          p['conv_b'][layer], p['w_ffn_down'][layer], conv_state[layer])
        new_conv.append(buf)
        h = h + rmsnorm(f, p['g_ffn_post'][layer])
    return h, k_sh, v_sh, logf_sh, jnp.stack(new_ret), jnp.stack(new_conv)


def setup_inputs(seed: int = 0) -> dict:
    key = jax.random.key(seed)
    ks = jax.random.split(key, 32)
    n_pages = PAST_LEN // PAGE_SIZE
    n_phys = (DEC_BATCH * n_pages * 5) // 4

    def nrm(k, shape, scale=1.0):
        return scale * jax.random.normal(k, shape, jnp.float32)

    fox_bias_base = jnp.linspace(1.0, 6.0, H_FOX, dtype=jnp.float32)
    x_prompt = nrm(ks[0], (BATCH, SEQ, D_MODEL))
    x_sample = nrm(ks[1], (DEC_BATCH, DEC_SEQ, D_MODEL))
    mem_prompt = nrm(ks[2], (BATCH, N_MEM, D_MODEL))
    cache_k = nrm(ks[3], (n_phys, PAGE_SIZE, H_FOX, DH))
    cache_v = nrm(ks[4], (n_phys, PAGE_SIZE, H_FOX, DH))
    cache_logf = jax.nn.log_sigmoid(fox_bias_base + nrm(ks[5], (n_phys, PAGE_SIZE, H_FOX)))
    page_table = jax.random.permutation(ks[6], n_phys)[:DEC_BATCH * n_pages].reshape(
        DEC_BATCH, n_pages).astype(jnp.int32)
    cache_mem_k = nrm(ks[7], (DEPTH, DEC_BATCH, N_MEM, H_MEM, DH))
    cache_mem_v = nrm(ks[8], (DEPTH, DEC_BATCH, N_MEM, H_MEM, DH))
    state_ret = nrm(ks[9], (N_A, DEC_BATCH, H_RET, DK_RET, DV_RET))
    state_conv = nrm(ks[10], (DEPTH, DEC_BATCH, CONV_W - 1, 2 * D_FF))
    g_mix_pre = 1.0 + nrm(ks[11], (DEPTH, D_MODEL), 0.05)
    g_mix_post = 1.0 + nrm(ks[12], (DEPTH, D_MODEL), 0.05)
    g_ffn_pre = 1.0 + nrm(ks[13], (DEPTH, D_MODEL), 0.05)
    g_ffn_post = 1.0 + nrm(ks[14], (DEPTH, D_MODEL), 0.05)
    w_in_a = nrm(ks[15], (N_A, D_MODEL, D_IN_A), D_MODEL ** -0.5)
    w_out_a = nrm(ks[16], (N_A, D_CAT_A, D_MODEL), D_CAT_A ** -0.5)
    w_in_b = nrm(ks[17], (N_B, D_MODEL, D_IN_B), D_MODEL ** -0.5)
    w_out_b = nrm(ks[18], (N_B, D_CAT_B, D_MODEL), D_CAT_B ** -0.5)
    w_mem_kv = nrm(ks[19], (DEPTH, D_MODEL, 2 * D_MEMQ), D_MODEL ** -0.5)
    g_kv = 1.0 + nrm(ks[20], (D_MODEL,), 0.05)
    w_kv_shared = nrm(ks[21], (D_MODEL, D_KV_SH), D_MODEL ** -0.5)
    b_f = fox_bias_base + nrm(ks[22], (H_FOX,), 0.1)
    w_ffn_up = nrm(ks[23], (DEPTH, D_MODEL, 2 * D_FF), D_MODEL ** -0.5)
    conv_w = nrm(ks[24], (DEPTH, CONV_W, 2 * D_FF), CONV_W ** -0.5)
    conv_b = nrm(ks[25], (DEPTH, 2 * D_FF), 0.02)
    w_ffn_down = nrm(ks[26], (DEPTH, D_FF, D_MODEL), D_FF ** -0.5)
    return {'x_prompt': x_prompt, 'x_sample': x_sample, 'mem_prompt': mem_prompt,
            'cache_k': cache_k, 'cache_v': cache_v, 'cache_logf': cache_logf, 'page_table': page_table,
            'cache_mem_k': cache_mem_k, 'cache_mem_v': cache_mem_v,
            'state_ret': state_ret, 'state_conv': state_conv,
            'g_mix_pre': g_mix_pre, 'g_mix_post': g_mix_post, 'g_ffn_pre': g_ffn_pre, 'g_ffn_post': g_ffn_post,
            'w_in_a': w_in_a, 'w_out_a': w_out_a, 'w_in_b': w_in_b, 'w_out_b': w_out_b,
            'w_mem_kv': w_mem_kv, 'g_kv': g_kv, 'w_kv_shared': w_kv_shared, 'b_f': b_f,
            'w_ffn_up': w_ffn_up, 'conv_w': conv_w, 'conv_b': conv_b, 'w_ffn_down': w_ffn_down}


def reference(x_prompt, x_sample, mem_prompt, cache_k, cache_v, cache_logf, page_table,
              cache_mem_k, cache_mem_v, state_ret, state_conv,
              g_mix_pre, g_mix_post, g_ffn_pre, g_ffn_post, w_in_a, w_out_a, w_in_b, w_out_b,
              w_mem_kv, g_kv, w_kv_shared, b_f, w_ffn_up, conv_w, conv_b, w_ffn_down):
    p = {'g_mix_pre': g_mix_pre, 'g_mix_post': g_mix_post, 'g_ffn_pre': g_ffn_pre, 'g_ffn_post': g_ffn_post,
         'w_in_a': w_in_a, 'w_out_a': w_out_a, 'w_in_b': w_in_b, 'w_out_b': w_out_b,
         'g_kv': g_kv, 'w_kv_shared': w_kv_shared, 'b_f': b_f,
         'w_ffn_up': w_ffn_up, 'conv_w': conv_w, 'conv_b': conv_b, 'w_ffn_down': w_ffn_down}

    B, L, _ = x_prompt.shape
    M = mem_prompt.shape[1]
    mkv = jnp.einsum('bmd,lde->lbme', mem_prompt, w_mem_kv)
    mem_k_p = mkv[..., :D_MEMQ].reshape(DEPTH, B, M, H_MEM, DH)
    mem_v_p = mkv[..., D_MEMQ:].reshape(DEPTH, B, M, H_MEM, DH)
    ret0 = jnp.zeros((N_A, B, H_RET, DK_RET, DV_RET), jnp.float32)
    conv0 = jnp.zeros((DEPTH, B, CONV_W - 1, 2 * D_FF), x_prompt.dtype)
    pos_p = jnp.arange(L, dtype=jnp.int32)
    y_prompt, k_p, v_p, logf_p, ret_p, conv_p = trunk(
        x_prompt, pos_p, mem_k_p, mem_v_p, ret0, conv0, None, None, None, p)

    DB, Ls, _ = x_sample.shape
    n_pages = page_table.shape[1]
    past_len = n_pages * cache_k.shape[1]
    past_k = cache_k[page_table].reshape(DB, past_len, H_FOX, DH)
    past_v = cache_v[page_table].reshape(DB, past_len, H_FOX, DH)
    past_logf = cache_logf[page_table].reshape(DB, past_len, H_FOX)
    pos_s = past_len + jnp.arange(Ls, dtype=jnp.int32)
    y_sample, k_s, v_s, logf_s, ret_s, conv_s = trunk(
        x_sample, pos_s, cache_mem_k, cache_mem_v, state_ret, state_conv, past_k, past_v, past_logf, p)

    return (y_prompt, y_sample, k_p, v_p, logf_p, k_s, v_s, logf_s,
            mem_k_p, mem_v_p, ret_p, ret_s, conv_p, conv_s)
```

```python
import functools

import jax
import jax.numpy as jnp
from jax import lax
from jax.experimental import pallas as pl
from jax.experimental.pallas import tpu as pltpu

F32 = jnp.float32
BF16 = jnp.bfloat16

D_MODEL = 1024
DEPTH = 4
N_A = 2
H_RET = 6
DK_RET = 64
DV_RET = 128
H_FOX = 12
DH = 64
H_MEM = 4
N_MEM = 256
D_RET_QK = H_RET * DK_RET
D_RET_V = H_RET * DV_RET
D_FOX = H_FOX * DH
D_MEMQ = H_MEM * DH
D_FF = 2816
RET_CHUNK = 128
ROPE_BASE = 10000.0
EPS = 1e-6

LANES = 128
HEAD_PAD = 16
VMEM_LIMIT = 48 * 1024 * 1024

NEG_INF = float("-inf")


def _params(sem):
    return pltpu.CompilerParams(dimension_semantics=sem, vmem_limit_bytes=VMEM_LIMIT)


def _rms(x, g):
    return x * lax.rsqrt(jnp.mean(x * x, axis=-1, keepdims=True) + EPS) * g


def _dot(a, b):
    return jnp.dot(a, b, preferred_element_type=F32)


def _dot_nt(a, b):
    return lax.dot_general(a, b, (((1,), (1,)), ((), ())), preferred_element_type=F32)


def _lane(shape):
    return lax.broadcasted_iota(jnp.int32, shape, len(shape) - 1)


def _pad_rows(x, n):
    if x.shape[0] == n:
        return x
    return jnp.concatenate([x, jnp.zeros((n - x.shape[0],) + x.shape[1:], x.dtype)], axis=0)


def _pick(n, cands):
    for c in cands:
        if n % c == 0:
            return c
    return n


def _nm_kernel(x_ref, g_ref, w_ref, o_ref, xn_ref, *, norm):
    @pl.when(pl.program_id(1) == 0)
    def _():
        x = x_ref[...]
        if norm:
            x = _rms(x, g_ref[...])
        xn_ref[...] = x.astype(BF16)

    o_ref[...] = _dot(xn_ref[...], w_ref[...].astype(BF16))


def norm_matmul(x, g, w, layer, *, norm=True):
    t, d = x.shape
    n = w.shape[-1]
    tm = _pick(t, (1024, 512, 256))
    tn = _pick(n, (512, 256, 128))
    return pl.pallas_call(
        functools.partial(_nm_kernel, norm=norm),
        grid=(t // tm, n // tn),
        in_specs=[
            pl.BlockSpec((tm, d), lambda i, j: (i, 0)),
            pl.BlockSpec((1, d), lambda i, j: (0, 0)),
            pl.BlockSpec((None, d, tn), lambda i, j: (layer, 0, j)),
        ],
        out_specs=pl.BlockSpec((tm, tn), lambda i, j: (i, j)),
        out_shape=jax.ShapeDtypeStruct((t, n), F32),
        scratch_shapes=[pltpu.VMEM((tm, d), BF16)],
        compiler_params=_params(("arbitrary", "arbitrary")),
        name="norm_matmul",
    )(x, g.reshape(1, d), w)


def _memkv_kernel(x_ref, w_ref, k_ref, v_ref):
    acc = _dot(x_ref[...].astype(BF16), w_ref[...].astype(BF16))
    k_ref[...] = acc[:, :D_MEMQ]
    v_ref[...] = acc[:, D_MEMQ:]


def mem_kv(x, w):
    t, d = x.shape
    tm = _pick(t, (1024, 512, 256))
    out = jax.ShapeDtypeStruct((DEPTH, t, D_MEMQ), F32)
    return pl.pallas_call(
        _memkv_kernel,
        grid=(t // tm, DEPTH),
        in_specs=[
            pl.BlockSpec((tm, d), lambda i, l: (i, 0)),
            pl.BlockSpec((None, d, 2 * D_MEMQ), lambda i, l: (l, 0, 0)),
        ],
        out_specs=[pl.BlockSpec((None, tm, D_MEMQ), lambda i, l: (l, i, 0))] * 2,
        out_shape=[out, out],
        compiler_params=_params(("arbitrary", "arbitrary")),
        name="mem_kv",
    )(x, w)


def _log_sigmoid(x):
    return jnp.minimum(x, 0.0) - jnp.log1p(jnp.exp(-jnp.abs(x)))


def _kv_kernel(x_ref, g_ref, wk_ref, wv_ref, wf_ref, bf_ref,
               k_ref, v_ref, kb_ref, vb_ref, lf_ref, wkb, wvb, wfb):
    @pl.when(pl.program_id(0) == 0)
    def _():
        wkb[...] = wk_ref[...].astype(BF16)
        wvb[...] = wv_ref[...].astype(BF16)
        wfb[...] = wf_ref[...].astype(BF16)

    xn = _rms(x_ref[...], g_ref[...]).astype(BF16)
    k = _dot(xn, wkb[...])
    v = _dot(xn, wvb[...])
    k_ref[...] = k
    v_ref[...] = v
    for p in range(D_FOX // LANES):
        kb_ref[p] = k[:, p * LANES:(p + 1) * LANES].astype(BF16)
        vb_ref[p] = v[:, p * LANES:(p + 1) * LANES].astype(BF16)
    lf_ref[...] = _log_sigmoid(_dot(xn, wfb[...]) + bf_ref[...])


def shared_kv(h, g, wk, wv, wf, bf):
    t, d = h.shape
    tm = _pick(t, (512, 256))
    npair = D_FOX // LANES
    full = lambda shape: pl.BlockSpec(shape, lambda i: (0,) * len(shape))
    return pl.pallas_call(
        _kv_kernel,
        grid=(t // tm,),
        in_specs=[
            pl.BlockSpec((tm, d), lambda i: (i, 0)),
            full((1, d)), full((d, D_FOX)), full((d, D_FOX)), full((d, LANES)), full((1, LANES)),
        ],
        out_specs=[
            pl.BlockSpec((tm, D_FOX), lambda i: (i, 0)),
            pl.BlockSpec((tm, D_FOX), lambda i: (i, 0)),
            pl.BlockSpec((npair, tm, LANES), lambda i: (0, i, 0)),
            pl.BlockSpec((npair, tm, LANES), lambda i: (0, i, 0)),
            pl.BlockSpec((tm, LANES), lambda i: (i, 0)),
        ],
        out_shape=[
            jax.ShapeDtypeStruct((t, D_FOX), F32),
            jax.ShapeDtypeStruct((t, D_FOX), F32),
            jax.ShapeDtypeStruct((npair, t, LANES), BF16),
            jax.ShapeDtypeStruct((npair, t, LANES), BF16),
            jax.ShapeDtypeStruct((t, LANES), F32),
        ],
        scratch_shapes=[pltpu.VMEM((d, D_FOX), BF16), pltpu.VMEM((d, D_FOX), BF16),
                        pltpu.VMEM((d, LANES), BF16)],
        compiler_params=_params(("arbitrary",)),
        name="shared_kv",
    )(h, g.reshape(1, d), wk, wv, wf, bf)


def _mnr_kernel(c1_ref, c2_ref, w_ref, g_ref, h_ref, o_ref, wb_ref):
    @pl.when(pl.program_id(0) == 0)
    def _():
        wb_ref[...] = w_ref[...].astype(BF16)

    n1 = c1_ref.shape[1]
    o = (_dot(c1_ref[...].astype(BF16), wb_ref[:n1, :])
         + _dot(c2_ref[...].astype(BF16), wb_ref[n1:, :]))
    o_ref[...] = h_ref[...] + _rms(o, g_ref[...])


def out_proj_residual(c1, c2, w, layer, g, h):
    t, d = h.shape
    n1, n2 = c1.shape[1], c2.shape[1]
    tm = _pick(t, (512, 256))
    return pl.pallas_call(
        _mnr_kernel,
        grid=(t // tm,),
        in_specs=[
            pl.BlockSpec((tm, n1), lambda i: (i, 0)),
            pl.BlockSpec((tm, n2), lambda i: (i, 0)),
            pl.BlockSpec((None, n1 + n2, d), lambda i: (layer, 0, 0)),
            pl.BlockSpec((1, d), lambda i: (0, 0)),
            pl.BlockSpec((tm, d), lambda i: (i, 0)),
        ],
        out_specs=pl.BlockSpec((tm, d), lambda i: (i, 0)),
        out_shape=jax.ShapeDtypeStruct((t, d), F32),
        scratch_shapes=[pltpu.VMEM((n1 + n2, d), BF16)],
        compiler_params=_params(("arbitrary",)),
        name="out_proj_residual",
    )(c1, c2, w, g.reshape(1, d), h)


def _gelu_tanh(x):
    return 0.5 * x * (1.0 + jnp.tanh(0.7978845608028654 * (x + 0.044715 * (x * x * x))))


def _ffn_kernel(*refs, tm, tf, nf, tiles_per_batch, seq_rows):
    (h_ref, gpre_ref, wg_ref, wv_ref, cwg_ref, cwv_ref, cbg_ref, cbv_ref, wd_ref, gpost_ref) = refs[:10]
    if seq_rows is None:
        o_ref, tg_ref, tv_ref, xn_ref, acc_ref, cg_ref, cv_ref = refs[10:]
    else:
        p1g_ref, p1v_ref, p2g_ref, p2v_ref, o_ref, tg_ref, tv_ref, xn_ref, acc_ref = refs[10:]
    i = pl.program_id(0)
    f = pl.program_id(1)

    @pl.when(f == 0)
    def _():
        xn_ref[...] = _rms(h_ref[...], gpre_ref[...]).astype(BF16)
        acc_ref[...] = jnp.zeros_like(acc_ref)

    xn = xn_ref[...]
    row = lax.broadcasted_iota(jnp.int32, (tm, tf), 0)

    def conv(w_ref, cw_ref, cb_ref, t_ref, extra):
        u = _dot(xn, w_ref[...].astype(BF16))
        u1 = pltpu.roll(u, 1, 0)
        u2 = pltpu.roll(u, 2, 0)
        if seq_rows is None:
            c_ref = extra

            @pl.when((i % tiles_per_batch) == 0)
            def _():
                c_ref[f] = jnp.zeros((8, tf), F32)

            prev = c_ref[f]
            u1 = jnp.where(row == 0, prev[7:8, :], u1)
            u2 = jnp.where(row == 0, prev[6:7, :], jnp.where(row == 1, prev[7:8, :], u2))
            c_ref[f] = u[tm - 8:, :]
            t_ref[0] = u[tm - 8:, :]
        else:
            p1_ref, p2_ref = extra
            r = row & (seq_rows - 1)
            u1 = jnp.where(r >= 1, u1, p1_ref[...])
            u2 = jnp.where(r >= 2, u2, p2_ref[...])
            t_ref[...] = u
        cw = cw_ref[...]
        return cb_ref[...] + cw[0:1, :] * u2 + cw[1:2, :] * u1 + cw[2:3, :] * u

    if seq_rows is None:
        gate = conv(wg_ref, cwg_ref, cbg_ref, tg_ref, cg_ref)
        val = conv(wv_ref, cwv_ref, cbv_ref, tv_ref, cv_ref)
    else:
        gate = conv(wg_ref, cwg_ref, cbg_ref, tg_ref, (p1g_ref, p2g_ref))
        val = conv(wv_ref, cwv_ref, cbv_ref, tv_ref, (p1v_ref, p2v_ref))
    hid = (_gelu_tanh(gate) * val).astype(BF16)
    acc_ref[...] += _dot(hid, wd_ref[...].astype(BF16))

    @pl.when(f == nf - 1)
    def _():
        o_ref[...] = h_ref[...] + _rms(acc_ref[...], gpost_ref[...])


def conv_ffn(h, layer, p, *, seq_len, conv_state=None):
    t, d = h.shape
    nb = t // seq_len
    tf = 256
    nf = D_FF // tf
    w_up, cw, cb, w_down = p['w_ffn_up'], p['conv_w'], p['conv_b'], p['w_ffn_down']
    cb3 = cb.reshape(DEPTH, 1, 2 * D_FF)
    if conv_state is None:
        tm = _pick(seq_len, (1024, 512, 256))
        tiles_per_batch = seq_len // tm
        seq_rows = None
        tail_shape = jax.ShapeDtypeStruct((t // tm, 8, D_FF), F32)
        tail_spec = pl.BlockSpec((1, 8, tf), lambda i, f: (i, 0, f))
        extra_in, extra_specs = [], []
        extra_scratch = [pltpu.VMEM((nf, 8, tf), F32), pltpu.VMEM((nf, 8, tf), F32)]
    else:
        tm = t
        tiles_per_batch = 1
        seq_rows = seq_len
        tail_shape = jax.ShapeDtypeStruct((t, D_FF), F32)
        tail_spec = pl.BlockSpec((tm, tf), lambda i, f: (i, f))
        st = conv_state
        z = jnp.zeros((nb, seq_len, 2 * D_FF), F32)
        prev1 = z.at[:, 0].set(st[:, 1]).reshape(t, 2 * D_FF)
        prev2 = z.at[:, 0].set(st[:, 0]).at[:, 1].set(st[:, 1]).reshape(t, 2 * D_FF)
        extra_in = [prev1, prev1, prev2, prev2]
        gspec = pl.BlockSpec((tm, tf), lambda i, f: (i, f))
        vspec = pl.BlockSpec((tm, tf), lambda i, f: (i, nf + f))
        extra_specs = [gspec, vspec, gspec, vspec]
        extra_scratch = []
    kern = functools.partial(_ffn_kernel, tm=tm, tf=tf, nf=nf, tiles_per_batch=tiles_per_batch,
                             seq_rows=seq_rows)
    h_new, tail_g, tail_v = pl.pallas_call(
        kern,
        grid=(t // tm, nf),
        in_specs=[
            pl.BlockSpec((tm, d), lambda i, f: (i, 0)),
            pl.BlockSpec((1, d), lambda i, f: (0, 0)),
            pl.BlockSpec((None, d, tf), lambda i, f: (layer, 0, f)),
            pl.BlockSpec((None, d, tf), lambda i, f: (layer, 0, nf + f)),
            pl.BlockSpec((None, 3, tf), lambda i, f: (layer, 0, f)),
            pl.BlockSpec((None, 3, tf), lambda i, f: (layer, 0, nf + f)),
            pl.BlockSpec((None, 1, tf), lambda i, f: (layer, 0, f)),
            pl.BlockSpec((None, 1, tf), lambda i, f: (layer, 0, nf + f)),
            pl.BlockSpec((None, tf, d), lambda i, f: (layer, f, 0)),
            pl.BlockSpec((1, d), lambda i, f: (0, 0)),
        ] + extra_specs,
        out_specs=[pl.BlockSpec((tm, d), lambda i, f: (i, 0)), tail_spec, tail_spec],
        out_shape=[jax.ShapeDtypeStruct((t, d), F32), tail_shape, tail_shape],
        scratch_shapes=[pltpu.VMEM((tm, d), BF16), pltpu.VMEM((tm, d), F32)] + extra_scratch,
        compiler_params=_params(("arbitrary", "arbitrary")),
        name="conv_ffn",
    )(h, p['g_ffn_pre'][layer].reshape(1, d), w_up, w_up, cw, cw, cb3, cb3, w_down,
      p['g_ffn_post'][layer].reshape(1, d), *extra_in)
    if conv_state is None:
        last = slice(tiles_per_batch - 1, None, tiles_per_batch)
        new_state = jnp.concatenate([tail_g[last, 6:8], tail_v[last, 6:8]], axis=-1)
    else:
        new_state = jnp.concatenate([tail_g.reshape(nb, seq_len, D_FF)[:, seq_len - 2:],
                                     tail_v.reshape(nb, seq_len, D_FF)[:, seq_len - 2:]], axis=-1)
    return h_new, new_state


def _rotary(x, cos, sins):
    parts = []
    for p in range(x.shape[1] // LANES):
        xb = x[:, p * LANES:(p + 1) * LANES]
        first = (_lane(xb.shape) & (DK_RET // 2)) == 0
        sw = jnp.where(first, pltpu.roll(xb, LANES - DK_RET // 2, 1), pltpu.roll(xb, DK_RET // 2, 1))
        parts.append(xb * cos[:, p * LANES:(p + 1) * LANES] + sw * sins[:, p * LANES:(p + 1) * LANES])
    return jnp.concatenate(parts, axis=1)


def _ret_kernel(q_ref, k_ref, v_ref, g_ref, cos_ref, sin_ref, dm_ref, qd_ref, kd_ref, sd_ref, s0_ref,
                r_ref, so_ref, s_ref, *, tq, nq):
    c = RET_CHUNK
    qi = pl.program_id(1)
    zero_half = jnp.zeros((DK_RET, DV_RET), F32)

    @pl.when(qi == 0)
    def _():
        for h in range(H_RET):
            halves = [zero_half, zero_half]
            halves[h % 2] = s0_ref[h]
            s_ref[h] = jnp.concatenate(halves, axis=0)

    cos = cos_ref[...]
    sins = sin_ref[...]
    q = _rotary(q_ref[...], cos, sins)
    k = _rotary(k_ref[...], cos, sins) * (DK_RET ** -0.5)
    rows = max(tq, c)
    for cs in range(0, rows, c):
        n = min(c, tq)
        qc = _pad_rows(q[cs:cs + n], c)
        kc = _pad_rows(k[cs:cs + n], c)
        vc = _pad_rows(v_ref[cs:cs + n, :], c)
        qd = qc * qd_ref[...]
        kd = kc * kd_ref[...]
        for p in range(H_RET // 2):
            sl = slice(p * LANES, (p + 1) * LANES)
            q2, k2, qd2, kd2 = qc[:, sl], kc[:, sl], qd[:, sl], kd[:, sl]
            k2b = k2.astype(BF16)
            hi = _lane(q2.shape) >= DK_RET
            for e in range(2):
                h = 2 * p + e
                mine = hi if e else jnp.logical_not(hi)
                qm = jnp.where(mine, q2, 0.0).astype(BF16)
                qdm = jnp.where(mine, qd2, 0.0).astype(BF16)
                kdm = jnp.where(mine, kd2, 0.0)
                vh = vc[:, h * DV_RET:(h + 1) * DV_RET].astype(BF16)
                inner = (_dot_nt(qm, k2b) * dm_ref[h]).astype(BF16)
                s_old = s_ref[h]
                o = _dot(inner, vh) + _dot(qdm, s_old.astype(BF16))
                s_ref[h] = sd_ref[h] * s_old + _dot(kdm.T.astype(BF16), vh)
                cen = o - jnp.mean(o, axis=-1, keepdims=True)
                y = cen * lax.rsqrt(jnp.mean(cen * cen, axis=-1, keepdims=True) + EPS)
                gh = g_ref[cs:cs + n, h * DV_RET:(h + 1) * DV_RET]
                r = (gh / (1.0 + jnp.exp(-gh))) * y[:n]
                r_ref[cs:cs + n, h * DV_RET:(h + 1) * DV_RET] = r.astype(r_ref.dtype)

    @pl.when(qi == nq - 1)
    def _():
        for h in range(H_RET):
            e = h % 2
            so_ref[h] = s_ref[h][e * DK_RET:(e + 1) * DK_RET, :]


def _ret_tables(chunk):
    c = RET_CHUNK
    log_gamma = jnp.log1p(-(2.0 ** (-5.0 - jnp.arange(H_RET, dtype=F32))))
    idx = jnp.arange(c, dtype=F32)
    rel = idx[:, None] - idx[None, :]
    dmask = jnp.where(rel >= 0, jnp.exp(log_gamma[:, None, None] * jnp.maximum(rel, 0.0)), 0.0)
    qdec = jnp.exp(log_gamma[None, :] * (idx[:, None] + 1.0))
    kdec = jnp.exp(log_gamma[None, :] * (chunk - 1.0 - idx[:, None]))
    sdec = jnp.exp(log_gamma * chunk)
    qdec = jnp.repeat(qdec, DK_RET, axis=1)
    kdec = jnp.repeat(kdec, DK_RET, axis=1)
    sdec = jnp.broadcast_to(sdec[:, None, None], (H_RET, 1, DV_RET))
    return dmask, qdec, kdec, sdec


def _rope_tables(pos):
    half = DK_RET // 2
    inv = ROPE_BASE ** (-jnp.arange(half, dtype=F32) / half)
    ang = pos.astype(F32)[:, None] * inv[None, :]
    cos, sin = jnp.cos(ang), jnp.sin(ang)
    cos_t = jnp.tile(jnp.concatenate([cos, cos], axis=1), (1, H_RET))
    sin_t = jnp.tile(jnp.concatenate([-sin, sin], axis=1), (1, H_RET))
    return cos_t, sin_t


def retention(z, s0, pos, *, seq_len, out_dtype):
    t = z.shape[0]
    nb = t // seq_len
    chunk = RET_CHUNK if seq_len % RET_CHUNK == 0 else seq_len
    tq = _pick(seq_len, (512, 256, 128))
    nq = seq_len // tq
    dmask, qdec, kdec, sdec = _ret_tables(chunk)
    cos_t, sin_t = _rope_tables(pos)
    nqk = D_RET_QK
    full = lambda shape: pl.BlockSpec(shape, lambda b, i: (0,) * len(shape))
    return pl.pallas_call(
        functools.partial(_ret_kernel, tq=tq, nq=nq),
        grid=(nb, nq),
        in_specs=[
            pl.BlockSpec((tq, nqk), lambda b, i: (b * nq + i, 0)),
            pl.BlockSpec((tq, nqk), lambda b, i: (b * nq + i, 1)),
            pl.BlockSpec((tq, D_RET_V), lambda b, i: (b * nq + i, 1)),
            pl.BlockSpec((tq, D_RET_V), lambda b, i: (b * nq + i, 2)),
            pl.BlockSpec((tq, nqk), lambda b, i: (i, 0)),
            pl.BlockSpec((tq, nqk), lambda b, i: (i, 0)),
            full((H_RET, RET_CHUNK, RET_CHUNK)),
            full((RET_CHUNK, nqk)),
            full((RET_CHUNK, nqk)),
            full((H_RET, 1, DV_RET)),
            pl.BlockSpec((None, H_RET, DK_RET, DV_RET), lambda b, i: (b, 0, 0, 0)),
        ],
        out_specs=[
            pl.BlockSpec((tq, D_RET_V), lambda b, i: (b * nq + i, 0)),
            pl.BlockSpec((None, H_RET, DK_RET, DV_RET), lambda b, i: (b, 0, 0, 0)),
        ],
        out_shape=[jax.ShapeDtypeStruct((t, D_RET_V), out_dtype),
                   jax.ShapeDtypeStruct((nb, H_RET, DK_RET, DV_RET), F32)],
        scratch_shapes=[pltpu.VMEM((H_RET, 2 * DK_RET, DV_RET), F32)],
        compiler_params=_params(("arbitrary", "arbitrary")),
        name="retention",
    )(z, z, z, z, cos_t, sin_t, dmask, qdec, kdec, sdec, s0)


def _mem_kernel(q_ref, mk_ref, mv_ref, o_ref, *, tq):
    rows = max(tq, 16)
    q = _pad_rows(q_ref[...] * (DH ** -0.5), rows)
    for p in range(D_MEMQ // LANES):
        sl = slice(p * LANES, (p + 1) * LANES)
        q2 = q[:, sl]
        k2 = mk_ref[:, sl].astype(BF16)
        v2 = mv_ref[:, sl].astype(BF16)
        hi = _lane(q2.shape) >= DH
        outs = []
        for e in range(2):
            mine = hi if e else jnp.logical_not(hi)
            s = _dot_nt(jnp.where(mine, q2, 0.0).astype(BF16), k2)
            pe = jnp.exp(s - jnp.max(s, axis=-1, keepdims=True))
            outs.append(_dot(pe.astype(BF16), v2) / jnp.sum(pe, axis=-1, keepdims=True))
        o = jnp.where(hi, outs[1], outs[0])
        o_ref[:, sl] = o[:tq].astype(o_ref.dtype)


def mem_attend(z, col_block, mk, mv, *, seq_len, out_dtype):
    t = z.shape[0]
    nb = t // seq_len
    tq = _pick(seq_len, (512, 256, 128))
    nq = seq_len // tq
    return pl.pallas_call(
        functools.partial(_mem_kernel, tq=tq),
        grid=(nb, nq),
        in_specs=[
            pl.BlockSpec((tq, D_MEMQ), lambda b, i: (b * nq + i, col_block)),
            pl.BlockSpec((None, N_MEM, D_MEMQ), lambda b, i: (b, 0, 0)),
            pl.BlockSpec((None, N_MEM, D_MEMQ), lambda b, i: (b, 0, 0)),
        ],
        out_specs=pl.BlockSpec((tq, D_MEMQ), lambda b, i: (b * nq + i, 0)),
        out_shape=jax.ShapeDtypeStruct((t, D_MEMQ), out_dtype),
        compiler_params=_params(("arbitrary", "arbitrary")),
        name="mem_attend",
    )(z, mk, mv)


def _suffix_scan(x):
    lane = _lane(x.shape)
    t = x
    d = 1
    while d < LANES:
        t = t + jnp.where(lane + d < LANES, pltpu.roll(t, LANES - d, 1), 0.0)
        d *= 2
    return t


def _cumsum_kernel(*refs, paged):
    if paged:
        refs = refs[1:]
    new_ref, x_ref, fo_ref, fn_ref, carry_ref = refs

    @pl.when(pl.program_id(1) == 0)
    def _():
        xn = new_ref[...]
        inc = _suffix_scan(xn)
        fn_ref[...] = -(inc - xn)
        carry_ref[...] = jnp.broadcast_to(inc[:, 0:1], carry_ref.shape)

    x = x_ref[...]
    inc = _suffix_scan(x)
    carry = carry_ref[...]
    fo_ref[...] = -(carry + (inc - x))
    carry_ref[...] = carry + inc[:, 0:1]


def forget_suffix(logf_t, new_t, page_table=None):
    nb = new_t.shape[0]
    if page_table is None:
        npg = logf_t.shape[2] // LANES
        x_spec = pl.BlockSpec((None, HEAD_PAD, LANES), lambda b, p: (b, 0, npg - 1 - p))
        grid_spec = dict(num_scalar_prefetch=0)
        args = (new_t, logf_t)
    else:
        npg = page_table.shape[1]
        x_spec = pl.BlockSpec((None, HEAD_PAD, LANES), lambda b, p, pt: (pt[b, npg - 1 - p], 0, 0))
        grid_spec = dict(num_scalar_prefetch=1)
        args = (page_table, new_t, logf_t)
    paged = page_table is not None
    new_spec = pl.BlockSpec((None, HEAD_PAD, LANES), lambda b, p, *_: (b, 0, 0))
    out_spec = pl.BlockSpec((None, HEAD_PAD, LANES), lambda b, p, *_: (b, 0, npg - 1 - p))
    return pl.pallas_call(
        functools.partial(_cumsum_kernel, paged=paged),
        grid_spec=pltpu.PrefetchScalarGridSpec(
            grid=(nb, npg), in_specs=[new_spec, x_spec], out_specs=[out_spec, new_spec],
            scratch_shapes=[pltpu.VMEM((HEAD_PAD, LANES), F32)], **grid_spec),
        out_shape=[jax.ShapeDtypeStruct((nb, HEAD_PAD, npg * LANES), F32),
                   jax.ShapeDtypeStruct((nb, HEAD_PAD, LANES), F32)],
        compiler_params=_params(("arbitrary", "arbitrary")),
        name="forget_suffix",
    )(*args)


def _fox_p_kernel(q_ref, kb_ref, vb_ref, fq_ref, fk_ref, o_ref, qs_ref, fqs_ref, m_ref, l_ref, acc_ref,
                  *, tq, tk):
    qi = pl.program_id(1)
    ki = pl.program_id(2)
    npair = H_FOX // 2
    lane_hi = _lane((tq, LANES)) >= DH

    @pl.when(ki == 0)
    def _():
        q = q_ref[...] * (DH ** -0.5)
        fq = fq_ref[...]
        for p in range(npair):
            q2 = q[:, p * LANES:(p + 1) * LANES]
            qs_ref[2 * p] = jnp.where(lane_hi, 0.0, q2).astype(BF16)
            qs_ref[2 * p + 1] = jnp.where(lane_hi, q2, 0.0).astype(BF16)
        for h in range(H_FOX):
            fqs_ref[h] = fq[:, h:h + 1]
        m_ref[...] = jnp.full(m_ref.shape, NEG_INF, F32)
        l_ref[...] = jnp.zeros_like(l_ref)
        acc_ref[...] = jnp.zeros_like(acc_ref)

    def step(masked):
        if masked:
            keep = (lax.broadcasted_iota(jnp.int32, (tq, tk), 1)
                    <= lax.broadcasted_iota(jnp.int32, (tq, tk), 0))

        def body(p, carry):
            kp = kb_ref[p]
            vp = vb_ref[p]
            alphas, pvs = [], []
            for e in range(2):
                h = 2 * p + e
                s = _dot_nt(qs_ref[h], kp) + fqs_ref[h] - fk_ref[pl.ds(h, 1), :]
                if masked:
                    s = jnp.where(keep, s, NEG_INF)
                m_prev = m_ref[h]
                m_new = jnp.maximum(m_prev, jnp.max(s, axis=-1, keepdims=True))
                alpha = jnp.exp(m_prev - m_new)
                pe = jnp.exp(s - m_new)
                l_ref[h] = alpha * l_ref[h] + jnp.sum(pe, axis=-1, keepdims=True)
                m_ref[h] = m_new
                alphas.append(alpha)
                pvs.append(_dot(pe.astype(BF16), vp))
            acc_ref[p] = (jnp.where(lane_hi, alphas[1], alphas[0]) * acc_ref[p]
                          + jnp.where(lane_hi, pvs[1], pvs[0]))
            return carry

        lax.fori_loop(0, npair, body, 0)

    @pl.when(ki < qi)
    def _():
        step(False)

    @pl.when(ki == qi)
    def _():
        step(True)
        for p in range(npair):
            denom = jnp.where(lane_hi, l_ref[2 * p + 1], l_ref[2 * p])
            o_ref[:, p * LANES:(p + 1) * LANES] = (acc_ref[p] / denom).astype(o_ref.dtype)


def fox_prompt(z, kb, vb, fq, fk_t, *, seq_len):
    t = z.shape[0]
    nb = t // seq_len
    tq = tk = _pick(seq_len, (512, 256, 128))
    nq = seq_len // tq
    npair = H_FOX // 2
    return pl.pallas_call(
        functools.partial(_fox_p_kernel, tq=tq, tk=tk),
        grid=(nb, nq, nq),
        in_specs=[
            pl.BlockSpec((tq, D_FOX), lambda b, i, j: (b * nq + i, 0)),
            pl.BlockSpec((npair, tk, LANES), lambda b, i, j: (0, b * nq + jnp.minimum(i, j), 0)),
            pl.BlockSpec((npair, tk, LANES), lambda b, i, j: (0, b * nq + jnp.minimum(i, j), 0)),
            pl.BlockSpec((tq, HEAD_PAD), lambda b, i, j: (b * nq + i, 0)),
            pl.BlockSpec((None, HEAD_PAD, tk), lambda b, i, j: (b, 0, jnp.minimum(i, j))),
        ],
        out_specs=pl.BlockSpec((tq, D_FOX), lambda b, i, j: (b * nq + i, 0)),
        out_shape=jax.ShapeDtypeStruct((t, D_FOX), BF16),
        scratch_shapes=[
            pltpu.VMEM((H_FOX, tq, LANES), BF16),
            pltpu.VMEM((H_FOX, tq, 1), F32),
            pltpu.VMEM((H_FOX, tq, 1), F32),
            pltpu.VMEM((H_FOX, tq, 1), F32),
            pltpu.VMEM((npair, tq, LANES), F32),
        ],
        compiler_params=_params(("arbitrary", "arbitrary", "arbitrary")),
        name="fox_prompt",
    )(z, kb, vb, fq, fk_t)


def _fox_s_kernel(pt_ref, q_ref, kn_ref, vn_ref, fq_ref, fkn_ref, fk_ref, *rest, npg, nj, ls):
    k_refs = rest[:npg]
    v_refs = rest[npg:2 * npg]
    o_ref, qbd_ref, m_ref, l_ref, acc_ref = rest[2 * npg:]
    j = pl.program_id(1)
    nrow = H_FOX * ls

    def update(s_raw, fk, mask, vs):
        fq = fq_ref[...]
        s = jnp.concatenate(
            [s_raw[h * ls:(h + 1) * ls, :] + fq[h * ls:(h + 1) * ls, :] - fk[h:h + 1, :]
             for h in range(H_FOX)], axis=0)
        if mask is not None:
            s = jnp.where(mask, s, NEG_INF)
        m_prev = m_ref[...]
        m_new = jnp.maximum(m_prev, jnp.max(s, axis=-1, keepdims=True))
        alpha = jnp.exp(m_prev - m_new)
        pe = jnp.exp(s - m_new)
        l_ref[...] = alpha * l_ref[...] + jnp.sum(pe, axis=-1, keepdims=True)
        m_ref[...] = m_new
        pb = pe.astype(BF16)
        pv = _dot(pb[:, :LANES], vs[0])
        for u in range(1, len(vs)):
            pv = pv + _dot(pb[:, u * LANES:(u + 1) * LANES], vs[u])
        acc_ref[...] = alpha * acc_ref[...] + pv

    @pl.when(j == 0)
    def _():
        q = q_ref[...] * (DH ** -0.5)
        head = _lane(q.shape) >> 6
        qbd_ref[...] = jnp.concatenate(
            [jnp.where(head == h, q, 0.0) for h in range(H_FOX)], axis=0).astype(BF16)
        m_ref[...] = jnp.full(m_ref.shape, NEG_INF, F32)
        l_ref[...] = jnp.zeros_like(l_ref)
        acc_ref[...] = jnp.zeros_like(acc_ref)
        kn = _pad_rows(kn_ref[...], LANES).astype(BF16)
        vn = _pad_rows(vn_ref[...], LANES).astype(BF16)
        col = _lane((nrow, LANES))
        qrow = lax.broadcasted_iota(jnp.int32, (nrow, LANES), 0) & (ls - 1)
        update(_dot_nt(qbd_ref[...], kn), fkn_ref[...], col <= qrow, [vn])

    qbd = qbd_ref[...]
    s_raw = jnp.concatenate([_dot_nt(qbd, k_refs[u][...].astype(BF16)) for u in range(npg)], axis=1)
    update(s_raw, fk_ref[...], None, [v_refs[u][...].astype(BF16) for u in range(npg)])

    @pl.when(j == nj - 1)
    def _():
        acc = acc_ref[...] / l_ref[...]
        head = _lane((ls, D_FOX)) >> 6
        out = jnp.zeros((ls, D_FOX), F32)
        for h in range(H_FOX):
            out = out + jnp.where(head == h, acc[h * ls:(h + 1) * ls, :], 0.0)
        o_ref[...] = out


def fox_sample(z, k_new, v_new, fq_col, fk_new_t, fk_t, cache_k, cache_v, page_table, *, seq_len):
    t = z.shape[0]
    ls = seq_len
    nb = t // ls
    n_pages = page_table.shape[1]
    npg = 8
    nj = n_pages // npg
    page = cache_k.shape[1]
    nrow = H_FOX * ls

    def page_spec(u):
        return pl.BlockSpec((None, page, D_FOX), lambda b, j, pt: (pt[b, j * npg + u], 0, 0))

    row_spec = pl.BlockSpec((ls, D_FOX), lambda b, j, pt: (b, 0))
    return pl.pallas_call(
        functools.partial(_fox_s_kernel, npg=npg, nj=nj, ls=ls),
        grid_spec=pltpu.PrefetchScalarGridSpec(
            num_scalar_prefetch=1,
            grid=(nb, nj),
            in_specs=[
                row_spec, row_spec, row_spec,
                pl.BlockSpec((None, nrow, 1), lambda b, j, pt: (b, 0, 0)),
                pl.BlockSpec((None, HEAD_PAD, LANES), lambda b, j, pt: (b, 0, 0)),
                pl.BlockSpec((None, HEAD_PAD, npg * page), lambda b, j, pt: (b, 0, j)),
            ] + [page_spec(u) for u in range(npg)] * 2,
            out_specs=row_spec,
            scratch_shapes=[
                pltpu.VMEM((nrow, D_FOX), BF16),
                pltpu.VMEM((nrow, 1), F32),
                pltpu.VMEM((nrow, 1), F32),
                pltpu.VMEM((nrow, D_FOX), F32),
            ]),
        out_shape=jax.ShapeDtypeStruct((t, D_FOX), F32),
        compiler_params=_params(("arbitrary", "arbitrary")),
        name="fox_sample",
    )(page_table, z, k_new, v_new, fq_col, fk_new_t, fk_t,
      *([cache_k] * npg), *([cache_v] * npg))


def _heads_to_sublanes(logf, nb, seq_len):
    x = logf[:, :H_FOX].reshape(nb, seq_len, H_FOX).transpose(0, 2, 1)
    return jnp.pad(x, ((0, 0), (0, HEAD_PAD - H_FOX), (0, 0)))


def _trunk(x, pos, mem_k, mem_v, ret_state, conv_state, past, p):
    nb, seq_len, d = x.shape
    t = nb * seq_len
    sample = past is not None
    cat_dtype = F32 if sample else BF16
    h = x.reshape(t, d)
    new_ret, new_conv = [], []
    k_sh = v_sh = logf_sh = None
    for layer in range(DEPTH):
        if layer < N_A:
            z = norm_matmul(h, p['g_mix_pre'][layer], p['w_in_a'], layer)
            s0 = (jnp.zeros((nb, H_RET, DK_RET, DV_RET), F32) if ret_state is None else ret_state[layer])
            r, s_new = retention(z, s0, pos, seq_len=seq_len, out_dtype=cat_dtype)
            new_ret.append(s_new)
            m = mem_attend(z, (2 * D_RET_QK + 2 * D_RET_V) // D_MEMQ, mem_k[layer], mem_v[layer],
                           seq_len=seq_len, out_dtype=cat_dtype)
            h = out_proj_residual(r, m, p['w_out_a'], layer, p['g_mix_post'][layer], h)
        else:
            if layer == N_A:
                k_sh, v_sh, kb, vb, lf = shared_kv(h, p['g_kv'], p['wk'], p['wv'], p['wf'], p['bf'])
                logf_sh = lf[:, :H_FOX]
                lf_t = _heads_to_sublanes(lf, nb, seq_len)
                if not sample:
                    f_t, _ = forget_suffix(lf_t, jnp.zeros((nb, HEAD_PAD, LANES), F32))
                    fq = f_t.transpose(0, 2, 1).reshape(t, HEAD_PAD)
                else:
                    cache_k, cache_v, cache_lf_t, page_table = past
                    new_t = jnp.pad(lf_t, ((0, 0), (0, 0), (0, LANES - seq_len)))
                    f_t, f_new_t = forget_suffix(cache_lf_t, new_t, page_table)
                    fq_col = f_new_t[:, :H_FOX, :seq_len].reshape(nb, H_FOX * seq_len, 1)
            j = layer - N_A
            z = norm_matmul(h, p['g_mix_pre'][layer], p['w_in_b'], j)
            if not sample:
                o = fox_prompt(z, kb, vb, fq, f_t, seq_len=seq_len)
            else:
                o = fox_sample(z, k_sh, v_sh, fq_col, f_new_t, f_t, cache_k, cache_v, page_table,
                               seq_len=seq_len)
            m = mem_attend(z, D_FOX // D_MEMQ, mem_k[layer], mem_v[layer],
                           seq_len=seq_len, out_dtype=cat_dtype)
            h = out_proj_residual(o, m, p['w_out_b'], j, p['g_mix_post'][layer], h)
        h, cs = conv_ffn(h, layer, p, seq_len=seq_len,
                         conv_state=None if conv_state is None else conv_state[layer])
        new_conv.append(cs)
    y = h.reshape(nb, seq_len, d)
    k_o = k_sh.reshape(nb, seq_len, H_FOX, DH)
    v_o = v_sh.reshape(nb, seq_len, H_FOX, DH)
    logf_o = logf_sh.reshape(nb, seq_len, H_FOX)
    return y, k_o, v_o, logf_o, jnp.stack(new_ret), jnp.stack(new_conv)


def kernel(x_prompt, x_sample, mem_prompt, cache_k, cache_v, cache_logf, page_table, cache_mem_k, cache_mem_v, state_ret, state_conv, g_mix_pre, g_mix_post, g_ffn_pre, g_ffn_post, w_in_a, w_out_a, w_in_b, w_out_b, w_mem_kv, g_kv, w_kv_shared, b_f, w_ffn_up, conv_w, conv_b, w_ffn_down):
    wf = jnp.pad(w_kv_shared[:, 2 * D_FOX:], ((0, 0), (0, LANES - H_FOX)))
    bf = jnp.pad(b_f, (0, LANES - H_FOX)).reshape(1, LANES)
    p = {'g_mix_pre': g_mix_pre, 'g_mix_post': g_mix_post, 'g_ffn_pre': g_ffn_pre, 'g_ffn_post': g_ffn_post,
         'w_in_a': w_in_a, 'w_out_a': w_out_a, 'w_in_b': w_in_b, 'w_out_b': w_out_b,
         'g_kv': g_kv, 'wk': w_kv_shared[:, :D_FOX], 'wv': w_kv_shared[:, D_FOX:2 * D_FOX],
         'wf': wf, 'bf': bf,
         'w_ffn_up': w_ffn_up, 'conv_w': conv_w, 'conv_b': conv_b, 'w_ffn_down': w_ffn_down}

    nb, seq_len, d = x_prompt.shape
    n_mem = mem_prompt.shape[1]
    mk, mv = mem_kv(mem_prompt.reshape(nb * n_mem, d), w_mem_kv)
    mk = mk.reshape(DEPTH, nb, n_mem, D_MEMQ)
    mv = mv.reshape(DEPTH, nb, n_mem, D_MEMQ)
    pos_p = jnp.arange(seq_len, dtype=jnp.int32)
    y_p, k_p, v_p, logf_p, ret_p, conv_p = _trunk(x_prompt, pos_p, mk, mv, None, None, None, p)
    mem_k_p = mk.reshape(DEPTH, nb, n_mem, H_MEM, DH)
    mem_v_p = mv.reshape(DEPTH, nb, n_mem, H_MEM, DH)

    db, ls, _ = x_sample.shape
    n_phys, page = cache_k.shape[:2]
    past_len = page_table.shape[1] * page
    pos_s = past_len + jnp.arange(ls, dtype=jnp.int32)
    cache_lf_t = jnp.pad(cache_logf.transpose(0, 2, 1), ((0, 0), (0, HEAD_PAD - H_FOX), (0, 0)))
    past = (cache_k.reshape(n_phys, page, D_FOX), cache_v.reshape(n_phys, page, D_FOX), cache_lf_t,
            page_table)
    y_s, k_s, v_s, logf_s, ret_s, conv_s = _trunk(
        x_sample, pos_s, cache_mem_k.reshape(DEPTH, db, N_MEM, D_MEMQ),
        cache_mem_v.reshape(DEPTH, db, N_MEM, D_MEMQ), state_ret, state_conv, past, p)

    return (y_p, y_s, k_p, v_p, logf_p, k_s, v_s, logf_s,
            mem_k_p, mem_v_p, ret_p, ret_s, conv_p, conv_s)
```

```python
import functools

import jax
import jax.numpy as jnp
from jax import lax
from jax.experimental import pallas as pl
from jax.experimental.pallas import tpu as pltpu

F32 = jnp.float32
BF16 = jnp.bfloat16

D_MODEL = 1024
DEPTH = 4
N_A = 2
H_RET = 6
DK_RET = 64
DV_RET = 128
H_FOX = 12
DH = 64
H_MEM = 4
N_MEM = 256
D_RET_QK = H_RET * DK_RET
D_RET_V = H_RET * DV_RET
D_FOX = H_FOX * DH
D_MEMQ = H_MEM * DH
D_FF = 2816
RET_CHUNK = 128
ROPE_BASE = 10000.0
EPS = 1e-6

LANES = 128
HEAD_PAD = 16
VMEM_LIMIT = 48 * 1024 * 1024

NEG_INF = float("-inf")


def _params(sem):
    return pltpu.CompilerParams(dimension_semantics=sem, vmem_limit_bytes=VMEM_LIMIT)


def _rms(x, g):
    return x * lax.rsqrt(jnp.mean(x * x, axis=-1, keepdims=True) + EPS) * g


def _dot(a, b):
    return jnp.dot(a, b, preferred_element_type=F32)


def _dot_nt(a, b):
    return lax.dot_general(a, b, (((1,), (1,)), ((), ())), preferred_element_type=F32)


def _lane(shape):
    return lax.broadcasted_iota(jnp.int32, shape, len(shape) - 1)


def _pad_rows(x, n):
    if x.shape[0] == n:
        return x
    return jnp.concatenate([x, jnp.zeros((n - x.shape[0],) + x.shape[1:], x.dtype)], axis=0)


def _pick(n, cands):
    for c in cands:
        if n % c == 0:
            return c
    return n


def _nm_kernel(x_ref, g_ref, w_ref, o_ref, xn_ref, *, norm):
    @pl.when(pl.program_id(1) == 0)
    def _():
        x = x_ref[...]
        if norm:
            x = _rms(x, g_ref[...])
        xn_ref[...] = x.astype(BF16)

    o_ref[...] = _dot(xn_ref[...], w_ref[...])


def norm_matmul(x, g, w, layer, *, norm=True):
    t, d = x.shape
    n = w.shape[-1]
    tm = _pick(t, (1024, 512, 256))
    tn = _pick(n, (512, 256, 128))
    return pl.pallas_call(
        functools.partial(_nm_kernel, norm=norm),
        grid=(t // tm, n // tn),
        in_specs=[
            pl.BlockSpec((tm, d), lambda i, j: (i, 0)),
            pl.BlockSpec((1, d), lambda i, j: (0, 0)),
            pl.BlockSpec((None, d, tn), lambda i, j: (layer, 0, j)),
        ],
        out_specs=pl.BlockSpec((tm, tn), lambda i, j: (i, j)),
        out_shape=jax.ShapeDtypeStruct((t, n), F32),
        scratch_shapes=[pltpu.VMEM((tm, d), BF16)],
        compiler_params=_params(("arbitrary", "arbitrary")),
        name="norm_matmul",
    )(x, g.reshape(1, d), w)


def _memkv_kernel(x_ref, w_ref, k_ref, v_ref):
    acc = _dot(x_ref[...].astype(BF16), w_ref[...])
    k_ref[...] = acc[:, :D_MEMQ]
    v_ref[...] = acc[:, D_MEMQ:]


def mem_kv(x, w):
    t, d = x.shape
    tm = _pick(t, (1024, 512, 256))
    out = jax.ShapeDtypeStruct((DEPTH, t, D_MEMQ), F32)
    return pl.pallas_call(
        _memkv_kernel,
        grid=(t // tm, DEPTH),
        in_specs=[
            pl.BlockSpec((tm, d), lambda i, l: (i, 0)),
            pl.BlockSpec((None, d, 2 * D_MEMQ), lambda i, l: (l, 0, 0)),
        ],
        out_specs=[pl.BlockSpec((None, tm, D_MEMQ), lambda i, l: (l, i, 0))] * 2,
        out_shape=[out, out],
        compiler_params=_params(("arbitrary", "arbitrary")),
        name="mem_kv",
    )(x, w)


def _log_sigmoid(x):
    return jnp.minimum(x, 0.0) - jnp.log1p(jnp.exp(-jnp.abs(x)))


def _kv_kernel(x_ref, g_ref, wk_ref, wv_ref, wf_ref, bf_ref, k_ref, v_ref, kb_ref, vb_ref, lf_ref):
    xn = _rms(x_ref[...], g_ref[...]).astype(BF16)
    k = _dot(xn, wk_ref[...])
    v = _dot(xn, wv_ref[...])
    k_ref[...] = k
    v_ref[...] = v
    for p in range(D_FOX // LANES):
        kb_ref[p] = k[:, p * LANES:(p + 1) * LANES].astype(BF16)
        vb_ref[p] = v[:, p * LANES:(p + 1) * LANES].astype(BF16)
    lf_ref[...] = _log_sigmoid(_dot(xn, wf_ref[...]) + bf_ref[...])


def shared_kv(h, g, wk, wv, wf, bf):
    t, d = h.shape
    tm = _pick(t, (512, 256))
    npair = D_FOX // LANES
    full = lambda shape: pl.BlockSpec(shape, lambda i: (0,) * len(shape))
    return pl.pallas_call(
        _kv_kernel,
        grid=(t // tm,),
        in_specs=[
            pl.BlockSpec((tm, d), lambda i: (i, 0)),
            full((1, d)), full((d, D_FOX)), full((d, D_FOX)), full((d, LANES)), full((1, LANES)),
        ],
        out_specs=[
            pl.BlockSpec((tm, D_FOX), lambda i: (i, 0)),
            pl.BlockSpec((tm, D_FOX), lambda i: (i, 0)),
            pl.BlockSpec((npair, tm, LANES), lambda i: (0, i, 0)),
            pl.BlockSpec((npair, tm, LANES), lambda i: (0, i, 0)),
            pl.BlockSpec((tm, LANES), lambda i: (i, 0)),
        ],
        out_shape=[
            jax.ShapeDtypeStruct((t, D_FOX), F32),
            jax.ShapeDtypeStruct((t, D_FOX), F32),
            jax.ShapeDtypeStruct((npair, t, LANES), BF16),
            jax.ShapeDtypeStruct((npair, t, LANES), BF16),
            jax.ShapeDtypeStruct((t, LANES), F32),
        ],
        compiler_params=_params(("arbitrary",)),
        name="shared_kv",
    )(h, g.reshape(1, d), wk, wv, wf, bf)


def _mnr_kernel(c1_ref, c2_ref, w_ref, g_ref, h_ref, o_ref):
    n1 = c1_ref.shape[1]
    o = (_dot(c1_ref[...].astype(BF16), w_ref[:n1, :])
         + _dot(c2_ref[...].astype(BF16), w_ref[n1:, :]))
    o_ref[...] = h_ref[...] + _rms(o, g_ref[...])


def out_proj_residual(c1, c2, w, layer, g, h):
    t, d = h.shape
    n1, n2 = c1.shape[1], c2.shape[1]
    tm = _pick(t, (512, 256))
    return pl.pallas_call(
        _mnr_kernel,
        grid=(t // tm,),
        in_specs=[
            pl.BlockSpec((tm, n1), lambda i: (i, 0)),
            pl.BlockSpec((tm, n2), lambda i: (i, 0)),
            pl.BlockSpec((None, n1 + n2, d), lambda i: (layer, 0, 0)),
            pl.BlockSpec((1, d), lambda i: (0, 0)),
            pl.BlockSpec((tm, d), lambda i: (i, 0)),
        ],
        out_specs=pl.BlockSpec((tm, d), lambda i: (i, 0)),
        out_shape=jax.ShapeDtypeStruct((t, d), F32),
        compiler_params=_params(("arbitrary",)),
        name="out_proj_residual",
    )(c1, c2, w, g.reshape(1, d), h)


FFN_COLS = 1408
FFN_ROWS = 16


def _gelu_tanh(x):
    return 0.5 * x * (1.0 + jnp.tanh(0.7978845608028654 * (x + 0.044715 * (x * x * x))))


def _ffn_kernel(*refs, tm, tf, nf, tiles_per_batch, seq_rows):
    (h_ref, gpre_ref, wg_ref, wv_ref, cwg_ref, cwv_ref, cbg_ref, cbv_ref, wd_ref, gpost_ref) = refs[:10]
    if seq_rows is None:
        o_ref, tg_ref, tv_ref, xn_ref, acc_ref, ug_ref, uv_ref, hid_ref, cg_ref, cv_ref = refs[10:]
        p1g_ref = p1v_ref = p2g_ref = p2v_ref = None
    else:
        (p1g_ref, p1v_ref, p2g_ref, p2v_ref, o_ref, tg_ref, tv_ref,
         xn_ref, acc_ref, ug_ref, uv_ref, hid_ref) = refs[10:]
        cg_ref = cv_ref = None
    i = pl.program_id(0)
    f = pl.program_id(1)
    rg = FFN_ROWS

    @pl.when(f == 0)
    def _():
        xn_ref[...] = _rms(h_ref[...], gpre_ref[...]).astype(BF16)

    xn = xn_ref[...]

    def up(w_ref, u_ref, t_ref, c_ref):
        if seq_rows is None:
            @pl.when((i % tiles_per_batch) == 0)
            def _():
                c_ref[f] = jnp.zeros((8, tf), F32)

            u_ref[0:8, :] = c_ref[f]
        else:
            u_ref[0:8, :] = jnp.zeros((8, tf), F32)
        u_ref[8:tm + 8, :] = _dot(xn, w_ref[...])
        if seq_rows is None:
            c_ref[f] = u_ref[tm:tm + 8, :]
            t_ref[0] = u_ref[tm:tm + 8, :]
        else:
            t_ref[...] = u_ref[8:tm + 8, :]

    up(wg_ref, ug_ref, tg_ref, cg_ref)
    up(wv_ref, uv_ref, tv_ref, cv_ref)

    if seq_rows is not None:
        r = lax.broadcasted_iota(jnp.int32, (rg, LANES), 0) & (seq_rows - 1)
        keep1 = r >= 1
        keep2 = r >= 2

    def conv(u_ref, cw_ref, cb_ref, p1_ref, p2_ref, r0, sl):
        a = u_ref[pl.ds(r0, rg + 8), sl]
        u = a[8:]
        u1 = pltpu.roll(a, 1, 0)[8:]
        u2 = pltpu.roll(a, 2, 0)[8:]
        if seq_rows is not None:
            u1 = jnp.where(keep1, u1, p1_ref[pl.ds(r0, rg), sl])
            u2 = jnp.where(keep2, u2, p2_ref[pl.ds(r0, rg), sl])
        return cb_ref[:, sl] + cw_ref[0:1, sl] * u2 + cw_ref[1:2, sl] * u1 + cw_ref[2:3, sl] * u

    def body(g, carry):
        r0 = pl.multiple_of(g * rg, rg)
        for lt in range(tf // LANES):
            sl = slice(lt * LANES, (lt + 1) * LANES)
            gate = conv(ug_ref, cwg_ref, cbg_ref, p1g_ref, p2g_ref, r0, sl)
            val = conv(uv_ref, cwv_ref, cbv_ref, p1v_ref, p2v_ref, r0, sl)
            hid_ref[pl.ds(r0, rg), sl] = (_gelu_tanh(gate) * val).astype(BF16)
        return carry

    lax.fori_loop(0, tm // rg, body, 0)
    part = _dot(hid_ref[...], wd_ref[...])

    @pl.when(f == 0)
    def _():
        acc_ref[...] = part

    @pl.when(f > 0)
    def _():
        acc_ref[...] += part

    @pl.when(f == nf - 1)
    def _():
        o_ref[...] = h_ref[...] + _rms(acc_ref[...], gpost_ref[...])


def conv_ffn(h, layer, p, *, seq_len, conv_state=None):
    t, d = h.shape
    nb = t // seq_len
    tf = FFN_COLS
    nf = D_FF // tf
    w_up, cw, cb, w_down = p['w_ffn_up'], p['conv_w'], p['conv_b'], p['w_ffn_down']
    cb3 = cb.reshape(DEPTH, 1, 2 * D_FF)
    if conv_state is None:
        tm = _pick(seq_len, (512, 256))
        tiles_per_batch = seq_len // tm
        seq_rows = None
        tail_shape = jax.ShapeDtypeStruct((t // tm, 8, D_FF), F32)
        tail_spec = pl.BlockSpec((1, 8, tf), lambda i, f: (i, 0, f))
        extra_in, extra_specs = [], []
        extra_scratch = [pltpu.VMEM((nf, 8, tf), F32), pltpu.VMEM((nf, 8, tf), F32)]
    else:
        tm = t
        tiles_per_batch = 1
        seq_rows = seq_len
        tail_shape = jax.ShapeDtypeStruct((t, D_FF), F32)
        tail_spec = pl.BlockSpec((tm, tf), lambda i, f: (i, f))
        st = conv_state
        z = jnp.zeros((nb, seq_len, 2 * D_FF), F32)
        prev1 = z.at[:, 0].set(st[:, 1]).reshape(t, 2 * D_FF)
        prev2 = z.at[:, 0].set(st[:, 0]).at[:, 1].set(st[:, 1]).reshape(t, 2 * D_FF)
        extra_in = [prev1, prev1, prev2, prev2]
        gspec = pl.BlockSpec((tm, tf), lambda i, f: (i, f))
        vspec = pl.BlockSpec((tm, tf), lambda i, f: (i, nf + f))
        extra_specs = [gspec, vspec, gspec, vspec]
        extra_scratch = []
    kern = functools.partial(_ffn_kernel, tm=tm, tf=tf, nf=nf, tiles_per_batch=tiles_per_batch,
                             seq_rows=seq_rows)
    h_new, tail_g, tail_v = pl.pallas_call(
        kern,
        grid=(t // tm, nf),
        in_specs=[
            pl.BlockSpec((tm, d), lambda i, f: (i, 0)),
            pl.BlockSpec((1, d), lambda i, f: (0, 0)),
            pl.BlockSpec((None, d, tf), lambda i, f: (layer, 0, f)),
            pl.BlockSpec((None, d, tf), lambda i, f: (layer, 0, nf + f)),
            pl.BlockSpec((None, 3, tf), lambda i, f: (layer, 0, f)),
            pl.BlockSpec((None, 3, tf), lambda i, f: (layer, 0, nf + f)),
            pl.BlockSpec((None, 1, tf), lambda i, f: (layer, 0, f)),
            pl.BlockSpec((None, 1, tf), lambda i, f: (layer, 0, nf + f)),
            pl.BlockSpec((None, tf, d), lambda i, f: (layer, f, 0)),
            pl.BlockSpec((1, d), lambda i, f: (0, 0)),
        ] + extra_specs,
        out_specs=[pl.BlockSpec((tm, d), lambda i, f: (i, 0)), tail_spec, tail_spec],
        out_shape=[jax.ShapeDtypeStruct((t, d), F32), tail_shape, tail_shape],
        scratch_shapes=[pltpu.VMEM((tm, d), BF16), pltpu.VMEM((tm, d), F32),
                        pltpu.VMEM((tm + 8, tf), F32), pltpu.VMEM((tm + 8, tf), F32),
                        pltpu.VMEM((tm, tf), BF16)] + extra_scratch,
        compiler_params=_params(("arbitrary", "arbitrary")),
        name="conv_ffn",
    )(h, p['g_ffn_pre'][layer].reshape(1, d), w_up, w_up, cw, cw, cb3, cb3, w_down,
      p['g_ffn_post'][layer].reshape(1, d), *extra_in)
    if conv_state is None:
        last = slice(tiles_per_batch - 1, None, tiles_per_batch)
        new_state = jnp.concatenate([tail_g[last, 6:8], tail_v[last, 6:8]], axis=-1)
    else:
        new_state = jnp.concatenate([tail_g.reshape(nb, seq_len, D_FF)[:, seq_len - 2:],
                                     tail_v.reshape(nb, seq_len, D_FF)[:, seq_len - 2:]], axis=-1)
    return h_new, new_state


def _rotary(x, cos, sins):
    parts = []
    for p in range(x.shape[1] // LANES):
        xb = x[:, p * LANES:(p + 1) * LANES]
        first = (_lane(xb.shape) & (DK_RET // 2)) == 0
        sw = jnp.where(first, pltpu.roll(xb, LANES - DK_RET // 2, 1), pltpu.roll(xb, DK_RET // 2, 1))
        parts.append(xb * cos[:, p * LANES:(p + 1) * LANES] + sw * sins[:, p * LANES:(p + 1) * LANES])
    return jnp.concatenate(parts, axis=1)


def _ret_kernel(q_ref, k_ref, v_ref, g_ref, cos_ref, sin_ref, dm_ref, qd_ref, kd_ref, sd_ref, s0_ref,
                r_ref, so_ref, s_ref, *, tq, nq):
    c = RET_CHUNK
    qi = pl.program_id(1)
    zero_half = jnp.zeros((DK_RET, DV_RET), F32)

    @pl.when(qi == 0)
    def _():
        for h in range(H_RET):
            halves = [zero_half, zero_half]
            halves[h % 2] = s0_ref[h]
            s_ref[h] = jnp.concatenate(halves, axis=0)

    cos = cos_ref[...]
    sins = sin_ref[...]
    q = _rotary(q_ref[...], cos, sins)
    k = _rotary(k_ref[...], cos, sins) * (DK_RET ** -0.5)
    rows = max(tq, c)
    for cs in range(0, rows, c):
        n = min(c, tq)
        qc = _pad_rows(q[cs:cs + n], c)
        kc = _pad_rows(k[cs:cs + n], c)
        vc = _pad_rows(v_ref[cs:cs + n, :], c)
        qd = qc * qd_ref[...]
        kd = kc * kd_ref[...]
        for p in range(H_RET // 2):
            sl = slice(p * LANES, (p + 1) * LANES)
            q2, k2, qd2, kd2 = qc[:, sl], kc[:, sl], qd[:, sl], kd[:, sl]
            k2b = k2.astype(BF16)
            hi = _lane(q2.shape) >= DK_RET
            for e in range(2):
                h = 2 * p + e
                mine = hi if e else jnp.logical_not(hi)
                qm = jnp.where(mine, q2, 0.0).astype(BF16)
                qdm = jnp.where(mine, qd2, 0.0).astype(BF16)
                kdm = jnp.where(mine, kd2, 0.0)
                vh = vc[:, h * DV_RET:(h + 1) * DV_RET].astype(BF16)
                inner = (_dot_nt(qm, k2b) * dm_ref[h]).astype(BF16)
                s_old = s_ref[h]
                o = _dot(inner, vh) + _dot(qdm, s_old.astype(BF16))
                s_ref[h] = sd_ref[h] * s_old + _dot(kdm.T.astype(BF16), vh)
                cen = o - jnp.mean(o, axis=-1, keepdims=True)
                y = cen * lax.rsqrt(jnp.mean(cen * cen, axis=-1, keepdims=True) + EPS)
                gh = g_ref[cs:cs + n, h * DV_RET:(h + 1) * DV_RET]
                r = (gh / (1.0 + jnp.exp(-gh))) * y[:n]
                r_ref[cs:cs + n, h * DV_RET:(h + 1) * DV_RET] = r.astype(r_ref.dtype)

    @pl.when(qi == nq - 1)
    def _():
        for h in range(H_RET):
            e = h % 2
            so_ref[h] = s_ref[h][e * DK_RET:(e + 1) * DK_RET, :]


def _ret_tables(chunk):
    c = RET_CHUNK
    log_gamma = jnp.log1p(-(2.0 ** (-5.0 - jnp.arange(H_RET, dtype=F32))))
    idx = jnp.arange(c, dtype=F32)
    rel = idx[:, None] - idx[None, :]
    dmask = jnp.where(rel >= 0, jnp.exp(log_gamma[:, None, None] * jnp.maximum(rel, 0.0)), 0.0)
    qdec = jnp.exp(log_gamma[None, :] * (idx[:, None] + 1.0))
    kdec = jnp.exp(log_gamma[None, :] * (chunk - 1.0 - idx[:, None]))
    sdec = jnp.exp(log_gamma * chunk)
    qdec = jnp.repeat(qdec, DK_RET, axis=1)
    kdec = jnp.repeat(kdec, DK_RET, axis=1)
    sdec = jnp.broadcast_to(sdec[:, None, None], (H_RET, 1, DV_RET))
    return dmask, qdec, kdec, sdec


def _rope_tables(pos):
    half = DK_RET // 2
    inv = ROPE_BASE ** (-jnp.arange(half, dtype=F32) / half)
    ang = pos.astype(F32)[:, None] * inv[None, :]
    cos, sin = jnp.cos(ang), jnp.sin(ang)
    cos_t = jnp.tile(jnp.concatenate([cos, cos], axis=1), (1, H_RET))
    sin_t = jnp.tile(jnp.concatenate([-sin, sin], axis=1), (1, H_RET))
    return cos_t, sin_t


def retention(z, s0, pos, *, seq_len, out_dtype):
    t = z.shape[0]
    nb = t // seq_len
    chunk = RET_CHUNK if seq_len % RET_CHUNK == 0 else seq_len
    tq = _pick(seq_len, (512, 256, 128))
    nq = seq_len // tq
    dmask, qdec, kdec, sdec = _ret_tables(chunk)
    cos_t, sin_t = _rope_tables(pos)
    nqk = D_RET_QK
    full = lambda shape: pl.BlockSpec(shape, lambda b, i: (0,) * len(shape))
    return pl.pallas_call(
        functools.partial(_ret_kernel, tq=tq, nq=nq),
        grid=(nb, nq),
        in_specs=[
            pl.BlockSpec((tq, nqk), lambda b, i: (b * nq + i, 0)),
            pl.BlockSpec((tq, nqk), lambda b, i: (b * nq + i, 1)),
            pl.BlockSpec((tq, D_RET_V), lambda b, i: (b * nq + i, 1)),
            pl.BlockSpec((tq, D_RET_V), lambda b, i: (b * nq + i, 2)),
            pl.BlockSpec((tq, nqk), lambda b, i: (i, 0)),
            pl.BlockSpec((tq, nqk), lambda b, i: (i, 0)),
            full((H_RET, RET_CHUNK, RET_CHUNK)),
            full((RET_CHUNK, nqk)),
            full((RET_CHUNK, nqk)),
            full((H_RET, 1, DV_RET)),
            pl.BlockSpec((None, H_RET, DK_RET, DV_RET), lambda b, i: (b, 0, 0, 0)),
        ],
        out_specs=[
            pl.BlockSpec((tq, D_RET_V), lambda b, i: (b * nq + i, 0)),
            pl.BlockSpec((None, H_RET, DK_RET, DV_RET), lambda b, i: (b, 0, 0, 0)),
        ],
        out_shape=[jax.ShapeDtypeStruct((t, D_RET_V), out_dtype),
                   jax.ShapeDtypeStruct((nb, H_RET, DK_RET, DV_RET), F32)],
        scratch_shapes=[pltpu.VMEM((H_RET, 2 * DK_RET, DV_RET), F32)],
        compiler_params=_params(("arbitrary", "arbitrary")),
        name="retention",
    )(z, z, z, z, cos_t, sin_t, dmask, qdec, kdec, sdec, s0)


def _mem_kernel(q_ref, mk_ref, mv_ref, o_ref, *, tq):
    rows = max(tq, 16)
    q = _pad_rows(q_ref[...] * (DH ** -0.5), rows)
    for p in range(D_MEMQ // LANES):
        sl = slice(p * LANES, (p + 1) * LANES)
        q2 = q[:, sl]
        k2 = mk_ref[:, sl].astype(BF16)
        v2 = mv_ref[:, sl].astype(BF16)
        hi = _lane(q2.shape) >= DH
        outs = []
        for e in range(2):
            mine = hi if e else jnp.logical_not(hi)
            s = _dot_nt(jnp.where(mine, q2, 0.0).astype(BF16), k2)
            pe = jnp.exp(s - jnp.max(s, axis=-1, keepdims=True))
            outs.append(_dot(pe.astype(BF16), v2) / jnp.sum(pe, axis=-1, keepdims=True))
        o = jnp.where(hi, outs[1], outs[0])
        o_ref[:, sl] = o[:tq].astype(o_ref.dtype)


def mem_attend(z, col_block, mk, mv, *, seq_len, out_dtype):
    t = z.shape[0]
    nb = t // seq_len
    tq = _pick(seq_len, (512, 256, 128))
    nq = seq_len // tq
    return pl.pallas_call(
        functools.partial(_mem_kernel, tq=tq),
        grid=(nb, nq),
        in_specs=[
            pl.BlockSpec((tq, D_MEMQ), lambda b, i: (b * nq + i, col_block)),
            pl.BlockSpec((None, N_MEM, D_MEMQ), lambda b, i: (b, 0, 0)),
            pl.BlockSpec((None, N_MEM, D_MEMQ), lambda b, i: (b, 0, 0)),
        ],
        out_specs=pl.BlockSpec((tq, D_MEMQ), lambda b, i: (b * nq + i, 0)),
        out_shape=jax.ShapeDtypeStruct((t, D_MEMQ), out_dtype),
        compiler_params=_params(("arbitrary", "arbitrary")),
        name="mem_attend",
    )(z, mk, mv)


def _suffix_scan(x):
    lane = _lane(x.shape)
    t = x
    d = 1
    while d < LANES:
        t = t + jnp.where(lane + d < LANES, pltpu.roll(t, LANES - d, 1), 0.0)
        d *= 2
    return t


def _suffix_scan_rows(x):
    n = x.shape[0]
    row = lax.broadcasted_iota(jnp.int32, x.shape, 0)
    t = x
    d = 1
    while d < n:
        t = t + jnp.where(row + d < n, pltpu.roll(t, n - d, 0), 0.0)
        d *= 2
    return t


def _cumsum_kernel(x_ref, fo_ref, carry_ref):
    @pl.when(pl.program_id(1) == 0)
    def _():
        carry_ref[...] = jnp.zeros_like(carry_ref)

    x = x_ref[...]
    inc = _suffix_scan(x)
    carry = carry_ref[...]
    fo_ref[...] = -(carry + (inc - x))
    carry_ref[...] = carry + inc[:, 0:1]


def forget_suffix(logf_t):
    nb, _, seq_len = logf_t.shape
    npg = seq_len // LANES
    spec = pl.BlockSpec((None, HEAD_PAD, LANES), lambda b, p: (b, 0, npg - 1 - p))
    return pl.pallas_call(
        _cumsum_kernel,
        grid=(nb, npg),
        in_specs=[spec],
        out_specs=spec,
        out_shape=jax.ShapeDtypeStruct((nb, HEAD_PAD, seq_len), F32),
        scratch_shapes=[pltpu.VMEM((HEAD_PAD, LANES), F32)],
        compiler_params=_params(("arbitrary", "arbitrary")),
        name="forget_suffix",
    )(logf_t)


FOX_ROWS = 16


def _fox_p_kernel(q_ref, kb_ref, vb_ref, fq_ref, fk_ref, o_ref,
                  qs_ref, fqs_ref, m_ref, l_ref, acc_ref, s_ref, p_ref, al_ref, *, tq, tk):
    qi = pl.program_id(1)
    ki = pl.program_id(2)
    npair = H_FOX // 2
    lane_hi = _lane((tq, LANES)) >= DH

    @pl.when(ki == 0)
    def _():
        q = q_ref[...] * (DH ** -0.5)
        fq = fq_ref[...]
        for p in range(npair):
            q2 = q[:, p * LANES:(p + 1) * LANES]
            qs_ref[2 * p] = jnp.where(lane_hi, 0.0, q2).astype(BF16)
            qs_ref[2 * p + 1] = jnp.where(lane_hi, q2, 0.0).astype(BF16)
        for h in range(H_FOX):
            fqs_ref[h] = fq[:, h:h + 1]
        m_ref[...] = jnp.full(m_ref.shape, NEG_INF, F32)
        l_ref[...] = jnp.zeros_like(l_ref)
        acc_ref[...] = jnp.zeros_like(acc_ref)

    rg = FOX_ROWS

    def step(masked):
        def body(p, carry):
            kp = kb_ref[p]
            vp = vb_ref[p]
            pvs = []
            for e in range(2):
                h = 2 * p + e
                s_ref[...] = _dot_nt(qs_ref[h], kp)
                fk = fk_ref[pl.ds(h, 1), :]
                for g in range(tq // rg):
                    r0 = g * rg
                    w = min(tk, LANES * ((r0 + rg + LANES - 1) // LANES)) if masked else tk
                    s = s_ref[r0:r0 + rg, 0:w] + fqs_ref[h, r0:r0 + rg, :] - fk[:, 0:w]
                    if masked:
                        col = lax.broadcasted_iota(jnp.int32, (rg, w), 1)
                        row = lax.broadcasted_iota(jnp.int32, (rg, w), 0) + r0
                        s = jnp.where(col <= row, s, NEG_INF)
                    m_prev = m_ref[h, r0:r0 + rg, :]
                    m_new = jnp.maximum(m_prev, jnp.max(s, axis=-1, keepdims=True))
                    alpha = jnp.exp(m_prev - m_new)
                    pe = jnp.exp(s - m_new)
                    l_ref[h, r0:r0 + rg, :] = (alpha * l_ref[h, r0:r0 + rg, :]
                                               + jnp.sum(pe, axis=-1, keepdims=True))
                    m_ref[h, r0:r0 + rg, :] = m_new
                    al_ref[e, r0:r0 + rg, :] = alpha
                    p_ref[r0:r0 + rg, 0:w] = pe.astype(BF16)
                    if w < tk:
                        p_ref[r0:r0 + rg, w:tk] = jnp.zeros((rg, tk - w), BF16)
                pvs.append(_dot(p_ref[...], vp))
            acc_ref[p] = (jnp.where(lane_hi, al_ref[1], al_ref[0]) * acc_ref[p]
                          + jnp.where(lane_hi, pvs[1], pvs[0]))
            return carry

        lax.fori_loop(0, npair, body, 0)

    @pl.when(ki < qi)
    def _():
        step(False)

    @pl.when(ki == qi)
    def _():
        step(True)
        for p in range(npair):
            denom = jnp.where(lane_hi, l_ref[2 * p + 1], l_ref[2 * p])
            o_ref[:, p * LANES:(p + 1) * LANES] = (acc_ref[p] / denom).astype(o_ref.dtype)


def fox_prompt(z, kb, vb, fq, fk_t, *, seq_len):
    t = z.shape[0]
    nb = t // seq_len
    tq = tk = _pick(seq_len, (512, 256, 128))
    nq = seq_len // tq
    npair = H_FOX // 2
    return pl.pallas_call(
        functools.partial(_fox_p_kernel, tq=tq, tk=tk),
        grid=(nb, nq, nq),
        in_specs=[
            pl.BlockSpec((tq, D_FOX), lambda b, i, j: (b * nq + i, 0)),
            pl.BlockSpec((npair, tk, LANES), lambda b, i, j: (0, b * nq + jnp.minimum(i, j), 0)),
            pl.BlockSpec((npair, tk, LANES), lambda b, i, j: (0, b * nq + jnp.minimum(i, j), 0)),
            pl.BlockSpec((tq, HEAD_PAD), lambda b, i, j: (b * nq + i, 0)),
            pl.BlockSpec((None, HEAD_PAD, tk), lambda b, i, j: (b, 0, jnp.minimum(i, j))),
        ],
        out_specs=pl.BlockSpec((tq, D_FOX), lambda b, i, j: (b * nq + i, 0)),
        out_shape=jax.ShapeDtypeStruct((t, D_FOX), BF16),
        scratch_shapes=[
            pltpu.VMEM((H_FOX, tq, LANES), BF16),
            pltpu.VMEM((H_FOX, tq, 1), F32),
            pltpu.VMEM((H_FOX, tq, 1), F32),
            pltpu.VMEM((H_FOX, tq, 1), F32),
            pltpu.VMEM((npair, tq, LANES), F32),
            pltpu.VMEM((tq, tk), F32),
            pltpu.VMEM((tq, tk), BF16),
            pltpu.VMEM((2, tq, 1), F32),
        ],
        compiler_params=_params(("arbitrary", "arbitrary", "arbitrary")),
        name="fox_prompt",
    )(z, kb, vb, fq, fk_t)


def _fox_s_kernel(pt_ref, q_ref, kn_ref, vn_ref, lfr_ref, lfn_ref, *rest, npg, nj, ls):
    lf_refs = rest[:npg]
    k_refs = rest[npg:2 * npg]
    v_refs = rest[2 * npg:3 * npg]
    o_ref, qbd_ref, fq_ref, carry_ref, m_ref, l_ref, acc_ref = rest[3 * npg:]
    j = pl.program_id(1)
    nrow = H_FOX * ls

    def update(s_raw, fk, mask, vs, v_transposed):
        fq = fq_ref[...]
        s = jnp.concatenate(
            [s_raw[h * ls:(h + 1) * ls, :] + fq[h * ls:(h + 1) * ls, :] - fk[h:h + 1, :]
             for h in range(H_FOX)], axis=0)
        if mask is not None:
            s = jnp.where(mask, s, NEG_INF)
        m_prev = m_ref[...]
        m_new = jnp.maximum(m_prev, jnp.max(s, axis=-1, keepdims=True))
        alpha = jnp.exp(m_prev - m_new)
        pe = jnp.exp(s - m_new)
        l_ref[...] = alpha * l_ref[...] + jnp.sum(pe, axis=-1, keepdims=True)
        m_ref[...] = m_new
        pb = pe.astype(BF16)
        mm = _dot_nt if v_transposed else _dot
        pv = mm(pb[:, :LANES], vs[0])
        for u in range(1, len(vs)):
            pv = pv + mm(pb[:, u * LANES:(u + 1) * LANES], vs[u])
        acc_ref[...] = alpha * acc_ref[...] + pv

    @pl.when(j == 0)
    def _():
        q = q_ref[...] * (DH ** -0.5)
        head = _lane(q.shape) >> 6
        qbd_ref[...] = jnp.concatenate(
            [jnp.where(head == h, q, 0.0) for h in range(H_FOX)], axis=0).astype(BF16)
        m_ref[...] = jnp.full(m_ref.shape, NEG_INF, F32)
        l_ref[...] = jnp.zeros_like(l_ref)
        acc_ref[...] = jnp.zeros_like(acc_ref)
        lfr = lfr_ref[...]
        fq_rows = -(_suffix_scan_rows(lfr) - lfr)
        fq_ref[...] = jnp.concatenate([fq_rows[:, h:h + 1] for h in range(H_FOX)], axis=0)
        lfn = lfn_ref[...]
        inc = _suffix_scan(lfn)
        carry_ref[...] = jnp.broadcast_to(inc[:, 0:1], carry_ref.shape)
        kn = _pad_rows(kn_ref[...], LANES).astype(BF16)
        vn = _pad_rows(vn_ref[...], LANES).astype(BF16)
        col = _lane((nrow, LANES))
        qrow = lax.broadcasted_iota(jnp.int32, (nrow, LANES), 0) & (ls - 1)
        update(_dot_nt(qbd_ref[...], kn), -(inc - lfn), col <= qrow, [vn], False)

    carry = carry_ref[...]
    fks = []
    for u in range(npg):
        x = lf_refs[u][...]
        inc = _suffix_scan(x)
        fks.append(-(carry + (inc - x)))
        carry = carry + inc[:, 0:1]
    carry_ref[...] = carry
    qbd = qbd_ref[...]
    s_raw = jnp.concatenate([_dot(qbd, k_refs[u][...].astype(BF16)) for u in range(npg)], axis=1)
    update(s_raw, jnp.concatenate(fks, axis=1), None,
           [v_refs[u][...].astype(BF16) for u in range(npg)], True)

    @pl.when(j == nj - 1)
    def _():
        acc = acc_ref[...] / l_ref[...]
        head = _lane((ls, D_FOX)) >> 6
        out = jnp.zeros((ls, D_FOX), F32)
        for h in range(H_FOX):
            out = out + jnp.where(head == h, acc[h * ls:(h + 1) * ls, :], 0.0)
        o_ref[...] = out


def fox_sample(z, k_new, v_new, lf_rows, lf_new_t, cache_lf_t, cache_kt, cache_vt, page_table, *, seq_len):
    t = z.shape[0]
    ls = seq_len
    nb = t // ls
    n_pages = page_table.shape[1]
    npg = 8
    nj = n_pages // npg
    page = cache_kt.shape[2]
    nrow = H_FOX * ls

    def page_spec(u, rows):
        return pl.BlockSpec((None, rows, page),
                            lambda b, j, pt: (pt[b, n_pages - 1 - (j * npg + u)], 0, 0))

    row_spec = pl.BlockSpec((ls, D_FOX), lambda b, j, pt: (b, 0))
    return pl.pallas_call(
        functools.partial(_fox_s_kernel, npg=npg, nj=nj, ls=ls),
        grid_spec=pltpu.PrefetchScalarGridSpec(
            num_scalar_prefetch=1,
            grid=(nb, nj),
            in_specs=[
                row_spec, row_spec, row_spec,
                pl.BlockSpec((ls, LANES), lambda b, j, pt: (b, 0)),
                pl.BlockSpec((None, HEAD_PAD, LANES), lambda b, j, pt: (b, 0, 0)),
            ] + [page_spec(u, HEAD_PAD) for u in range(npg)]
              + [page_spec(u, D_FOX) for u in range(npg)] * 2,
            out_specs=row_spec,
            scratch_shapes=[
                pltpu.VMEM((nrow, D_FOX), BF16),
                pltpu.VMEM((nrow, 1), F32),
                pltpu.VMEM((HEAD_PAD, LANES), F32),
                pltpu.VMEM((nrow, 1), F32),
                pltpu.VMEM((nrow, 1), F32),
                pltpu.VMEM((nrow, D_FOX), F32),
            ]),
        out_shape=jax.ShapeDtypeStruct((t, D_FOX), F32),
        compiler_params=_params(("arbitrary", "arbitrary")),
        name="fox_sample",
    )(page_table, z, k_new, v_new, lf_rows, lf_new_t,
      *([cache_lf_t] * npg), *([cache_kt] * npg), *([cache_vt] * npg))


def _heads_to_sublanes(logf, nb, seq_len):
    x = logf[:, :H_FOX].reshape(nb, seq_len, H_FOX).transpose(0, 2, 1)
    return jnp.pad(x, ((0, 0), (0, HEAD_PAD - H_FOX), (0, 0)))


def _trunk(x, pos, mem_k, mem_v, ret_state, conv_state, past, p):
    nb, seq_len, d = x.shape
    t = nb * seq_len
    sample = past is not None
    cat_dtype = F32 if sample else BF16
    h = x.reshape(t, d)
    new_ret, new_conv = [], []
    k_sh = v_sh = logf_sh = None
    for layer in range(DEPTH):
        if layer < N_A:
            z = norm_matmul(h, p['g_mix_pre'][layer], p['w_in_a'], layer)
            s0 = (jnp.zeros((nb, H_RET, DK_RET, DV_RET), F32) if ret_state is None else ret_state[layer])
            r, s_new = retention(z, s0, pos, seq_len=seq_len, out_dtype=cat_dtype)
            new_ret.append(s_new)
            m = mem_attend(z, (2 * D_RET_QK + 2 * D_RET_V) // D_MEMQ, mem_k[layer], mem_v[layer],
                           seq_len=seq_len, out_dtype=cat_dtype)
            h = out_proj_residual(r, m, p['w_out_a'], layer, p['g_mix_post'][layer], h)
        else:
            if layer == N_A:
                k_sh, v_sh, kb, vb, lf = shared_kv(h, p['g_kv'], p['wk'], p['wv'], p['wf'], p['bf'])
                logf_sh = lf[:, :H_FOX]
                lf_t = _heads_to_sublanes(lf, nb, seq_len)
                if not sample:
                    f_t = forget_suffix(lf_t)
                    fq = f_t.transpose(0, 2, 1).reshape(t, HEAD_PAD)
                else:
                    cache_kt, cache_vt, cache_lf_t, page_table = past
                    lf_new_t = jnp.pad(lf_t, ((0, 0), (0, 0), (0, LANES - seq_len)))
            j = layer - N_A
            z = norm_matmul(h, p['g_mix_pre'][layer], p['w_in_b'], j)
            if not sample:
                o = fox_prompt(z, kb, vb, fq, f_t, seq_len=seq_len)
            else:
                o = fox_sample(z, k_sh, v_sh, lf, lf_new_t, cache_lf_t, cache_kt, cache_vt, page_table,
                               seq_len=seq_len)
            m = mem_attend(z, D_FOX // D_MEMQ, mem_k[layer], mem_v[layer],
                           seq_len=seq_len, out_dtype=cat_dtype)
            h = out_proj_residual(o, m, p['w_out_b'], j, p['g_mix_post'][layer], h)
        h, cs = conv_ffn(h, layer, p, seq_len=seq_len,
                         conv_state=None if conv_state is None else conv_state[layer])
        new_conv.append(cs)
    y = h.reshape(nb, seq_len, d)
    k_o = k_sh.reshape(nb, seq_len, H_FOX, DH)
    v_o = v_sh.reshape(nb, seq_len, H_FOX, DH)
    logf_o = logf_sh.reshape(nb, seq_len, H_FOX)
    return y, k_o, v_o, logf_o, jnp.stack(new_ret), jnp.stack(new_conv)


def kernel(x_prompt, x_sample, mem_prompt, cache_k, cache_v, cache_logf, page_table, cache_mem_k, cache_mem_v, state_ret, state_conv, g_mix_pre, g_mix_post, g_ffn_pre, g_ffn_post, w_in_a, w_out_a, w_in_b, w_out_b, w_mem_kv, g_kv, w_kv_shared, b_f, w_ffn_up, conv_w, conv_b, w_ffn_down):
    wf = jnp.pad(w_kv_shared[:, 2 * D_FOX:], ((0, 0), (0, LANES - H_FOX))).astype(BF16)
    bf = jnp.pad(b_f, (0, LANES - H_FOX)).reshape(1, LANES)
    p = {'g_mix_pre': g_mix_pre, 'g_mix_post': g_mix_post, 'g_ffn_pre': g_ffn_pre, 'g_ffn_post': g_ffn_post,
         'w_in_a': w_in_a.astype(BF16), 'w_out_a': w_out_a.astype(BF16),
         'w_in_b': w_in_b.astype(BF16), 'w_out_b': w_out_b.astype(BF16),
         'g_kv': g_kv, 'wk': w_kv_shared[:, :D_FOX].astype(BF16),
         'wv': w_kv_shared[:, D_FOX:2 * D_FOX].astype(BF16), 'wf': wf, 'bf': bf,
         'w_ffn_up': w_ffn_up.astype(BF16), 'conv_w': conv_w, 'conv_b': conv_b,
         'w_ffn_down': w_ffn_down.astype(BF16)}

    nb, seq_len, d = x_prompt.shape
    n_mem = mem_prompt.shape[1]
    mk, mv = mem_kv(mem_prompt.reshape(nb * n_mem, d), w_mem_kv.astype(BF16))
    mk = mk.reshape(DEPTH, nb, n_mem, D_MEMQ)
    mv = mv.reshape(DEPTH, nb, n_mem, D_MEMQ)
    pos_p = jnp.arange(seq_len, dtype=jnp.int32)
    y_p, k_p, v_p, logf_p, ret_p, conv_p = _trunk(x_prompt, pos_p, mk, mv, None, None, None, p)
    mem_k_p = mk.reshape(DEPTH, nb, n_mem, H_MEM, DH)
    mem_v_p = mv.reshape(DEPTH, nb, n_mem, H_MEM, DH)

    db, ls, _ = x_sample.shape
    n_phys, page = cache_k.shape[:2]
    past_len = page_table.shape[1] * page
    pos_s = past_len + jnp.arange(ls, dtype=jnp.int32)
    cache_lf_t = jnp.pad(cache_logf.transpose(0, 2, 1), ((0, 0), (0, HEAD_PAD - H_FOX), (0, 0)))
    cache_kt = cache_k.transpose(0, 2, 3, 1).reshape(n_phys, D_FOX, page)
    cache_vt = cache_v.transpose(0, 2, 3, 1).reshape(n_phys, D_FOX, page)
    past = (cache_kt, cache_vt, cache_lf_t, page_table)
    y_s, k_s, v_s, logf_s, ret_s, conv_s = _trunk(
        x_sample, pos_s, cache_mem_k.reshape(DEPTH, db, N_MEM, D_MEMQ),
        cache_mem_v.reshape(DEPTH, db, N_MEM, D_MEMQ), state_ret, state_conv, past, p)

    return (y_p, y_s, k_p, v_p, logf_p, k_s, v_s, logf_s,
            mem_k_p, mem_v_p, ret_p, ret_s, conv_p, conv_s)
```

```python
import functools

import jax
import jax.numpy as jnp
from jax import lax
from jax.experimental import pallas as pl
from jax.experimental.pallas import tpu as pltpu

F32 = jnp.float32
BF16 = jnp.bfloat16

D_MODEL = 1024
DEPTH = 4
N_A = 2
H_RET = 6
DK_RET = 64
DV_RET = 128
H_FOX = 12
DH = 64
H_MEM = 4
N_MEM = 256
D_RET_QK = H_RET * DK_RET
D_RET_V = H_RET * DV_RET
D_FOX = H_FOX * DH
D_MEMQ = H_MEM * DH
D_FF = 2816
RET_CHUNK = 128
ROPE_BASE = 10000.0
EPS = 1e-6

LANES = 128
HEAD_PAD = 16
VMEM_LIMIT = 48 * 1024 * 1024

NEG_INF = float("-inf")


def _params(sem):
    return pltpu.CompilerParams(dimension_semantics=sem, vmem_limit_bytes=VMEM_LIMIT)


def _rms(x, g):
    return x * lax.rsqrt(jnp.mean(x * x, axis=-1, keepdims=True) + EPS) * g


def _dot(a, b):
    return jnp.dot(a, b, preferred_element_type=F32)


def _dot_nt(a, b):
    return lax.dot_general(a, b, (((1,), (1,)), ((), ())), preferred_element_type=F32)


def _lane(shape):
    return lax.broadcasted_iota(jnp.int32, shape, len(shape) - 1)


def _pad_rows(x, n):
    if x.shape[0] == n:
        return x
    return jnp.concatenate([x, jnp.zeros((n - x.shape[0],) + x.shape[1:], x.dtype)], axis=0)


def _pick(n, cands):
    for c in cands:
        if n % c == 0:
            return c
    return n


def _nm_kernel(x_ref, g_ref, w_ref, o_ref, xn_ref, *, norm):
    @pl.when(pl.program_id(1) == 0)
    def _():
        x = x_ref[...]
        if norm:
            x = _rms(x, g_ref[...])
        xn_ref[...] = x.astype(BF16)

    o_ref[...] = _dot(xn_ref[...], w_ref[...])


def norm_matmul(x, g, w, layer, *, norm=True):
    t, d = x.shape
    n = w.shape[-1]
    tm = _pick(t, (1024, 512, 256))
    tn = _pick(n, (512, 256, 128))
    return pl.pallas_call(
        functools.partial(_nm_kernel, norm=norm),
        grid=(t // tm, n // tn),
        in_specs=[
            pl.BlockSpec((tm, d), lambda i, j: (i, 0)),
            pl.BlockSpec((1, d), lambda i, j: (0, 0)),
            pl.BlockSpec((None, d, tn), lambda i, j: (layer, 0, j)),
        ],
        out_specs=pl.BlockSpec((tm, tn), lambda i, j: (i, j)),
        out_shape=jax.ShapeDtypeStruct((t, n), F32),
        scratch_shapes=[pltpu.VMEM((tm, d), BF16)],
        compiler_params=_params(("arbitrary", "arbitrary")),
        name="norm_matmul",
    )(x, g.reshape(1, d), w)


def _memkv_kernel(x_ref, w_ref, k_ref, v_ref):
    acc = _dot(x_ref[...].astype(BF16), w_ref[...])
    k_ref[...] = acc[:, :D_MEMQ]
    v_ref[...] = acc[:, D_MEMQ:]


def mem_kv(x, w):
    t, d = x.shape
    tm = _pick(t, (1024, 512, 256))
    out = jax.ShapeDtypeStruct((DEPTH, t, D_MEMQ), F32)
    return pl.pallas_call(
        _memkv_kernel,
        grid=(t // tm, DEPTH),
        in_specs=[
            pl.BlockSpec((tm, d), lambda i, l: (i, 0)),
            pl.BlockSpec((None, d, 2 * D_MEMQ), lambda i, l: (l, 0, 0)),
        ],
        out_specs=[pl.BlockSpec((None, tm, D_MEMQ), lambda i, l: (l, i, 0))] * 2,
        out_shape=[out, out],
        compiler_params=_params(("arbitrary", "arbitrary")),
        name="mem_kv",
    )(x, w)


def _log_sigmoid(x):
    return jnp.minimum(x, 0.0) - jnp.log1p(jnp.exp(-jnp.abs(x)))


def _kv_kernel(x_ref, g_ref, wk_ref, wv_ref, wf_ref, bf_ref, k_ref, v_ref, lf_ref, *attn_refs):
    xn = _rms(x_ref[...], g_ref[...]).astype(BF16)
    k = _dot(xn, wk_ref[...])
    v = _dot(xn, wv_ref[...])
    k_ref[...] = k
    v_ref[...] = v
    lf_ref[...] = _log_sigmoid(_dot(xn, wf_ref[...]) + bf_ref[...])
    if attn_refs:
        kb_ref, vbt_ref = attn_refs
        for p in range(D_FOX // LANES):
            kb_ref[p] = k[:, p * LANES:(p + 1) * LANES].astype(BF16)
            vbt_ref[p] = v[:, p * LANES:(p + 1) * LANES].T.astype(BF16)


def shared_kv(h, g, wk, wv, wf, bf, *, seq_len, attn_copies):
    t, d = h.shape
    tm = _pick(t, (512, 256))
    npair = D_FOX // LANES
    full = lambda shape: pl.BlockSpec(shape, lambda i: (0,) * len(shape))
    out_specs = [
        pl.BlockSpec((tm, D_FOX), lambda i: (i, 0)),
        pl.BlockSpec((tm, D_FOX), lambda i: (i, 0)),
        pl.BlockSpec((tm, LANES), lambda i: (i, 0)),
    ]
    out_shape = [
        jax.ShapeDtypeStruct((t, D_FOX), F32),
        jax.ShapeDtypeStruct((t, D_FOX), F32),
        jax.ShapeDtypeStruct((t, LANES), F32),
    ]
    if attn_copies:
        tpb = seq_len // tm
        out_specs += [
            pl.BlockSpec((npair, tm, LANES), lambda i: (0, i, 0)),
            pl.BlockSpec((None, npair, LANES, tm), lambda i: (i // tpb, 0, 0, i % tpb)),
        ]
        out_shape += [
            jax.ShapeDtypeStruct((npair, t, LANES), BF16),
            jax.ShapeDtypeStruct((t // seq_len, npair, LANES, seq_len), BF16),
        ]
    return pl.pallas_call(
        _kv_kernel,
        grid=(t // tm,),
        in_specs=[
            pl.BlockSpec((tm, d), lambda i: (i, 0)),
            full((1, d)), full((d, D_FOX)), full((d, D_FOX)), full((d, LANES)), full((1, LANES)),
        ],
        out_specs=out_specs,
        out_shape=out_shape,
        compiler_params=_params(("arbitrary",)),
        name="shared_kv",
    )(h, g.reshape(1, d), wk, wv, wf, bf)


def _mnr_kernel(c1_ref, c2_ref, w_ref, g_ref, h_ref, o_ref):
    n1 = c1_ref.shape[1]
    o = (_dot(c1_ref[...].astype(BF16), w_ref[:n1, :])
         + _dot(c2_ref[...].astype(BF16), w_ref[n1:, :]))
    o_ref[...] = h_ref[...] + _rms(o, g_ref[...])


def out_proj_residual(c1, c2, w, layer, g, h):
    t, d = h.shape
    n1, n2 = c1.shape[1], c2.shape[1]
    tm = _pick(t, (512, 256))
    return pl.pallas_call(
        _mnr_kernel,
        grid=(t // tm,),
        in_specs=[
            pl.BlockSpec((tm, n1), lambda i: (i, 0)),
            pl.BlockSpec((tm, n2), lambda i: (i, 0)),
            pl.BlockSpec((None, n1 + n2, d), lambda i: (layer, 0, 0)),
            pl.BlockSpec((1, d), lambda i: (0, 0)),
            pl.BlockSpec((tm, d), lambda i: (i, 0)),
        ],
        out_specs=pl.BlockSpec((tm, d), lambda i: (i, 0)),
        out_shape=jax.ShapeDtypeStruct((t, d), F32),
        compiler_params=_params(("arbitrary",)),
        name="out_proj_residual",
    )(c1, c2, w, g.reshape(1, d), h)


FFN_COLS = 1408
FFN_ROWS = 16
FFN_CHUNK = 128


def _gelu_tanh(x):
    return 0.5 * x * (1.0 + jnp.tanh(0.7978845608028654 * (x + 0.044715 * (x * x * x))))


def _ffn_kernel(*refs, tm, tf, nf, tiles_per_batch, seq_rows):
    (h_ref, gpre_ref, wg_ref, wv_ref, cwg_ref, cwv_ref, cbg_ref, cbv_ref, wd_ref, gpost_ref) = refs[:10]
    if seq_rows is None:
        o_ref, tg_ref, tv_ref, xn_ref, acc_ref, ug_ref, uv_ref, hid_ref, cg_ref, cv_ref = refs[10:]
        p1g_ref = p1v_ref = p2g_ref = p2v_ref = None
    else:
        (p1g_ref, p1v_ref, p2g_ref, p2v_ref, o_ref, tg_ref, tv_ref,
         xn_ref, acc_ref, ug_ref, uv_ref, hid_ref) = refs[10:]
        cg_ref = cv_ref = None
    i = pl.program_id(0)
    f = pl.program_id(1)
    rg = FFN_ROWS

    ch = FFN_CHUNK

    @pl.when(f == 0)
    def _():
        xn_ref[...] = _rms(h_ref[...], gpre_ref[...]).astype(BF16)
        acc_ref[...] = jnp.zeros_like(acc_ref)

    for u_ref, c_ref in ((ug_ref, cg_ref), (uv_ref, cv_ref)):
        if seq_rows is None:
            @pl.when((i % tiles_per_batch) == 0)
            def _():
                c_ref[f] = jnp.zeros((8, tf), F32)

            u_ref[0:8, :] = c_ref[f]
        else:
            u_ref[0:8, :] = jnp.zeros((8, tf), F32)

    if seq_rows is not None:
        r = lax.broadcasted_iota(jnp.int32, (rg, LANES), 0) & (seq_rows - 1)
        keep1 = r >= 1
        keep2 = r >= 2

    def up(c):
        xc = xn_ref[c * ch:(c + 1) * ch, :]
        ug_ref[8 + c * ch:8 + (c + 1) * ch, :] = _dot(xc, wg_ref[...])
        uv_ref[8 + c * ch:8 + (c + 1) * ch, :] = _dot(xc, wv_ref[...])

    def conv(u_ref, cw_ref, cb_ref, p1_ref, p2_ref, r0, sl):
        a = u_ref[r0:r0 + rg + 8, sl]
        u = a[8:]
        u1 = pltpu.roll(a, 1, 0)[8:]
        u2 = pltpu.roll(a, 2, 0)[8:]
        if seq_rows is not None:
            u1 = jnp.where(keep1, u1, p1_ref[r0:r0 + rg, sl])
            u2 = jnp.where(keep2, u2, p2_ref[r0:r0 + rg, sl])
        return cb_ref[:, sl] + cw_ref[0:1, sl] * u2 + cw_ref[1:2, sl] * u1 + cw_ref[2:3, sl] * u

    def gate_mul(c):
        for r0 in range(c * ch, (c + 1) * ch, rg):
            for lt in range(tf // LANES):
                sl = slice(lt * LANES, (lt + 1) * LANES)
                gate = conv(ug_ref, cwg_ref, cbg_ref, p1g_ref, p2g_ref, r0, sl)
                val = conv(uv_ref, cwv_ref, cbv_ref, p1v_ref, p2v_ref, r0, sl)
                hid_ref[r0:r0 + rg, sl] = (_gelu_tanh(gate) * val).astype(BF16)

    def down(c):
        rows = slice(c * ch, (c + 1) * ch)
        acc_ref[rows, :] += _dot(hid_ref[rows, :], wd_ref[...])

    nch = tm // ch
    up(0)
    for c in range(nch):
        if c + 1 < nch:
            up(c + 1)
        gate_mul(c)
        down(c)

    if seq_rows is None:
        cg_ref[f] = ug_ref[tm:tm + 8, :]
        cv_ref[f] = uv_ref[tm:tm + 8, :]
        tg_ref[0] = ug_ref[tm:tm + 8, :]
        tv_ref[0] = uv_ref[tm:tm + 8, :]
    else:
        tg_ref[...] = ug_ref[8:tm + 8, :]
        tv_ref[...] = uv_ref[8:tm + 8, :]

    @pl.when(f == nf - 1)
    def _():
        o_ref[...] = h_ref[...] + _rms(acc_ref[...], gpost_ref[...])


def conv_ffn(h, layer, p, *, seq_len, conv_state=None):
    t, d = h.shape
    nb = t // seq_len
    tf = FFN_COLS
    nf = D_FF // tf
    w_up, cw, cb, w_down = p['w_ffn_up'], p['conv_w'], p['conv_b'], p['w_ffn_down']
    cb3 = cb.reshape(DEPTH, 1, 2 * D_FF)
    if conv_state is None:
        tm = _pick(seq_len, (512, 256))
        tiles_per_batch = seq_len // tm
        seq_rows = None
        tail_shape = jax.ShapeDtypeStruct((t // tm, 8, D_FF), F32)
        tail_spec = pl.BlockSpec((1, 8, tf), lambda i, f: (i, 0, f))
        extra_in, extra_specs = [], []
        extra_scratch = [pltpu.VMEM((nf, 8, tf), F32), pltpu.VMEM((nf, 8, tf), F32)]
    else:
        tm = t
        tiles_per_batch = 1
        seq_rows = seq_len
        tail_shape = jax.ShapeDtypeStruct((t, D_FF), F32)
        tail_spec = pl.BlockSpec((tm, tf), lambda i, f: (i, f))
        st = conv_state
        z = jnp.zeros((nb, seq_len, 2 * D_FF), F32)
        prev1 = z.at[:, 0].set(st[:, 1]).reshape(t, 2 * D_FF)
        prev2 = z.at[:, 0].set(st[:, 0]).at[:, 1].set(st[:, 1]).reshape(t, 2 * D_FF)
        extra_in = [prev1, prev1, prev2, prev2]
        gspec = pl.BlockSpec((tm, tf), lambda i, f: (i, f))
        vspec = pl.BlockSpec((tm, tf), lambda i, f: (i, nf + f))
        extra_specs = [gspec, vspec, gspec, vspec]
        extra_scratch = []
    kern = functools.partial(_ffn_kernel, tm=tm, tf=tf, nf=nf, tiles_per_batch=tiles_per_batch,
                             seq_rows=seq_rows)
    h_new, tail_g, tail_v = pl.pallas_call(
        kern,
        grid=(t // tm, nf),
        in_specs=[
            pl.BlockSpec((tm, d), lambda i, f: (i, 0)),
            pl.BlockSpec((1, d), lambda i, f: (0, 0)),
            pl.BlockSpec((None, d, tf), lambda i, f: (layer, 0, f)),
            pl.BlockSpec((None, d, tf), lambda i, f: (layer, 0, nf + f)),
            pl.BlockSpec((None, 3, tf), lambda i, f: (layer, 0, f)),
            pl.BlockSpec((None, 3, tf), lambda i, f: (layer, 0, nf + f)),
            pl.BlockSpec((None, 1, tf), lambda i, f: (layer, 0, f)),
            pl.BlockSpec((None, 1, tf), lambda i, f: (layer, 0, nf + f)),
            pl.BlockSpec((None, tf, d), lambda i, f: (layer, f, 0)),
            pl.BlockSpec((1, d), lambda i, f: (0, 0)),
        ] + extra_specs,
        out_specs=[pl.BlockSpec((tm, d), lambda i, f: (i, 0)), tail_spec, tail_spec],
        out_shape=[jax.ShapeDtypeStruct((t, d), F32), tail_shape, tail_shape],
        scratch_shapes=[pltpu.VMEM((tm, d), BF16), pltpu.VMEM((tm, d), F32),
                        pltpu.VMEM((tm + 8, tf), F32), pltpu.VMEM((tm + 8, tf), F32),
                        pltpu.VMEM((tm, tf), BF16)] + extra_scratch,
        compiler_params=_params(("arbitrary", "arbitrary")),
        name="conv_ffn",
    )(h, p['g_ffn_pre'][layer].reshape(1, d), w_up, w_up, cw, cw, cb3, cb3, w_down,
      p['g_ffn_post'][layer].reshape(1, d), *extra_in)
    if conv_state is None:
        last = slice(tiles_per_batch - 1, None, tiles_per_batch)
        new_state = jnp.concatenate([tail_g[last, 6:8], tail_v[last, 6:8]], axis=-1)
    else:
        new_state = jnp.concatenate([tail_g.reshape(nb, seq_len, D_FF)[:, seq_len - 2:],
                                     tail_v.reshape(nb, seq_len, D_FF)[:, seq_len - 2:]], axis=-1)
    return h_new, new_state


def _rotary(x, cos, sins):
    parts = []
    for p in range(x.shape[1] // LANES):
        xb = x[:, p * LANES:(p + 1) * LANES]
        first = (_lane(xb.shape) & (DK_RET // 2)) == 0
        sw = jnp.where(first, pltpu.roll(xb, LANES - DK_RET // 2, 1), pltpu.roll(xb, DK_RET // 2, 1))
        parts.append(xb * cos[:, p * LANES:(p + 1) * LANES] + sw * sins[:, p * LANES:(p + 1) * LANES])
    return jnp.concatenate(parts, axis=1)


def _ret_kernel(q_ref, k_ref, v_ref, g_ref, cos_ref, sin_ref, dm_ref, qd_ref, kd_ref, sd_ref, s0_ref,
                r_ref, so_ref, s_ref, *, tq, nq):
    c = RET_CHUNK
    qi = pl.program_id(1)
    zero_half = jnp.zeros((DK_RET, DV_RET), F32)

    @pl.when(qi == 0)
    def _():
        for h in range(H_RET):
            halves = [zero_half, zero_half]
            halves[h % 2] = s0_ref[h]
            s_ref[h] = jnp.concatenate(halves, axis=0)

    cos = cos_ref[...]
    sins = sin_ref[...]
    q = _rotary(q_ref[...], cos, sins)
    k = _rotary(k_ref[...], cos, sins) * (DK_RET ** -0.5)
    rows = max(tq, c)
    for cs in range(0, rows, c):
        n = min(c, tq)
        qc = _pad_rows(q[cs:cs + n], c)
        kc = _pad_rows(k[cs:cs + n], c)
        vc = _pad_rows(v_ref[cs:cs + n, :], c)
        qd = qc * qd_ref[...]
        kd = kc * kd_ref[...]
        for p in range(H_RET // 2):
            sl = slice(p * LANES, (p + 1) * LANES)
            q2, k2, qd2, kd2 = qc[:, sl], kc[:, sl], qd[:, sl], kd[:, sl]
            k2b = k2.astype(BF16)
            hi = _lane(q2.shape) >= DK_RET
            for e in range(2):
                h = 2 * p + e
                mine = hi if e else jnp.logical_not(hi)
                qm = jnp.where(mine, q2, 0.0).astype(BF16)
                qdm = jnp.where(mine, qd2, 0.0).astype(BF16)
                kdm = jnp.where(mine, kd2, 0.0)
                vh = vc[:, h * DV_RET:(h + 1) * DV_RET].astype(BF16)
                inner = (_dot_nt(qm, k2b) * dm_ref[h]).astype(BF16)
                s_old = s_ref[h]
                o = _dot(inner, vh) + _dot(qdm, s_old.astype(BF16))
                s_ref[h] = sd_ref[h] * s_old + _dot(kdm.T.astype(BF16), vh)
                cen = o - jnp.mean(o, axis=-1, keepdims=True)
                y = cen * lax.rsqrt(jnp.mean(cen * cen, axis=-1, keepdims=True) + EPS)
                gh = g_ref[cs:cs + n, h * DV_RET:(h + 1) * DV_RET]
                r = (gh / (1.0 + jnp.exp(-gh))) * y[:n]
                r_ref[cs:cs + n, h * DV_RET:(h + 1) * DV_RET] = r.astype(r_ref.dtype)

    @pl.when(qi == nq - 1)
    def _():
        for h in range(H_RET):
            e = h % 2
            so_ref[h] = s_ref[h][e * DK_RET:(e + 1) * DK_RET, :]


def _ret_tables(chunk):
    c = RET_CHUNK
    log_gamma = jnp.log1p(-(2.0 ** (-5.0 - jnp.arange(H_RET, dtype=F32))))
    idx = jnp.arange(c, dtype=F32)
    rel = idx[:, None] - idx[None, :]
    dmask = jnp.where(rel >= 0, jnp.exp(log_gamma[:, None, None] * jnp.maximum(rel, 0.0)), 0.0)
    qdec = jnp.exp(log_gamma[None, :] * (idx[:, None] + 1.0))
    kdec = jnp.exp(log_gamma[None, :] * (chunk - 1.0 - idx[:, None]))
    sdec = jnp.exp(log_gamma * chunk)
    qdec = jnp.repeat(qdec, DK_RET, axis=1)
    kdec = jnp.repeat(kdec, DK_RET, axis=1)
    sdec = jnp.broadcast_to(sdec[:, None, None], (H_RET, 1, DV_RET))
    return dmask, qdec, kdec, sdec


def _rope_tables(pos):
    half = DK_RET // 2
    inv = ROPE_BASE ** (-jnp.arange(half, dtype=F32) / half)
    ang = pos.astype(F32)[:, None] * inv[None, :]
    cos, sin = jnp.cos(ang), jnp.sin(ang)
    cos_t = jnp.tile(jnp.concatenate([cos, cos], axis=1), (1, H_RET))
    sin_t = jnp.tile(jnp.concatenate([-sin, sin], axis=1), (1, H_RET))
    return cos_t, sin_t


def retention(z, s0, pos, *, seq_len, out_dtype):
    t = z.shape[0]
    nb = t // seq_len
    chunk = RET_CHUNK if seq_len % RET_CHUNK == 0 else seq_len
    tq = _pick(seq_len, (512, 256, 128))
    nq = seq_len // tq
    dmask, qdec, kdec, sdec = _ret_tables(chunk)
    cos_t, sin_t = _rope_tables(pos)
    nqk = D_RET_QK
    full = lambda shape: pl.BlockSpec(shape, lambda b, i: (0,) * len(shape))
    return pl.pallas_call(
        functools.partial(_ret_kernel, tq=tq, nq=nq),
        grid=(nb, nq),
        in_specs=[
            pl.BlockSpec((tq, nqk), lambda b, i: (b * nq + i, 0)),
            pl.BlockSpec((tq, nqk), lambda b, i: (b * nq + i, 1)),
            pl.BlockSpec((tq, D_RET_V), lambda b, i: (b * nq + i, 1)),
            pl.BlockSpec((tq, D_RET_V), lambda b, i: (b * nq + i, 2)),
            pl.BlockSpec((tq, nqk), lambda b, i: (i, 0)),
            pl.BlockSpec((tq, nqk), lambda b, i: (i, 0)),
            full((H_RET, RET_CHUNK, RET_CHUNK)),
            full((RET_CHUNK, nqk)),
            full((RET_CHUNK, nqk)),
            full((H_RET, 1, DV_RET)),
            pl.BlockSpec((None, H_RET, DK_RET, DV_RET), lambda b, i: (b, 0, 0, 0)),
        ],
        out_specs=[
            pl.BlockSpec((tq, D_RET_V), lambda b, i: (b * nq + i, 0)),
            pl.BlockSpec((None, H_RET, DK_RET, DV_RET), lambda b, i: (b, 0, 0, 0)),
        ],
        out_shape=[jax.ShapeDtypeStruct((t, D_RET_V), out_dtype),
                   jax.ShapeDtypeStruct((nb, H_RET, DK_RET, DV_RET), F32)],
        scratch_shapes=[pltpu.VMEM((H_RET, 2 * DK_RET, DV_RET), F32)],
        compiler_params=_params(("arbitrary", "arbitrary")),
        name="retention",
    )(z, z, z, z, cos_t, sin_t, dmask, qdec, kdec, sdec, s0)


def _mem_kernel(q_ref, mk_ref, mv_ref, o_ref, *, tq):
    rows = max(tq, 16)
    q = _pad_rows(q_ref[...] * (DH ** -0.5), rows)
    for p in range(D_MEMQ // LANES):
        sl = slice(p * LANES, (p + 1) * LANES)
        q2 = q[:, sl]
        k2 = mk_ref[:, sl].astype(BF16)
        v2 = mv_ref[:, sl].astype(BF16)
        hi = _lane(q2.shape) >= DH
        outs = []
        for e in range(2):
            mine = hi if e else jnp.logical_not(hi)
            s = _dot_nt(jnp.where(mine, q2, 0.0).astype(BF16), k2)
            pe = jnp.exp(s - jnp.max(s, axis=-1, keepdims=True))
            outs.append(_dot(pe.astype(BF16), v2) / jnp.sum(pe, axis=-1, keepdims=True))
        o = jnp.where(hi, outs[1], outs[0])
        o_ref[:, sl] = o[:tq].astype(o_ref.dtype)


def mem_attend(z, col_block, mk, mv, *, seq_len, out_dtype):
    t = z.shape[0]
    nb = t // seq_len
    tq = _pick(seq_len, (512, 256, 128))
    nq = seq_len // tq
    return pl.pallas_call(
        functools.partial(_mem_kernel, tq=tq),
        grid=(nb, nq),
        in_specs=[
            pl.BlockSpec((tq, D_MEMQ), lambda b, i: (b * nq + i, col_block)),
            pl.BlockSpec((None, N_MEM, D_MEMQ), lambda b, i: (b, 0, 0)),
            pl.BlockSpec((None, N_MEM, D_MEMQ), lambda b, i: (b, 0, 0)),
        ],
        out_specs=pl.BlockSpec((tq, D_MEMQ), lambda b, i: (b * nq + i, 0)),
        out_shape=jax.ShapeDtypeStruct((t, D_MEMQ), out_dtype),
        compiler_params=_params(("arbitrary", "arbitrary")),
        name="mem_attend",
    )(z, mk, mv)


def _suffix_scan(x):
    lane = _lane(x.shape)
    t = x
    d = 1
    while d < LANES:
        t = t + jnp.where(lane + d < LANES, pltpu.roll(t, LANES - d, 1), 0.0)
        d *= 2
    return t


def _suffix_scan_rows(x):
    n = x.shape[0]
    row = lax.broadcasted_iota(jnp.int32, x.shape, 0)
    t = x
    d = 1
    while d < n:
        t = t + jnp.where(row + d < n, pltpu.roll(t, n - d, 0), 0.0)
        d *= 2
    return t


def _cumsum_kernel(x_ref, fo_ref, carry_ref):
    @pl.when(pl.program_id(1) == 0)
    def _():
        carry_ref[...] = jnp.zeros_like(carry_ref)

    x = x_ref[...]
    inc = _suffix_scan(x)
    carry = carry_ref[...]
    fo_ref[...] = -(carry + (inc - x))
    carry_ref[...] = carry + inc[:, 0:1]


def forget_suffix(logf_t):
    nb, _, seq_len = logf_t.shape
    npg = seq_len // LANES
    spec = pl.BlockSpec((None, HEAD_PAD, LANES), lambda b, p: (b, 0, npg - 1 - p))
    return pl.pallas_call(
        _cumsum_kernel,
        grid=(nb, npg),
        in_specs=[spec],
        out_specs=spec,
        out_shape=jax.ShapeDtypeStruct((nb, HEAD_PAD, seq_len), F32),
        scratch_shapes=[pltpu.VMEM((HEAD_PAD, LANES), F32)],
        compiler_params=_params(("arbitrary", "arbitrary")),
        name="forget_suffix",
    )(logf_t)


SUBLANES = 8


def _fox_p_kernel(q_ref, kb_ref, vt_ref, fqr_ref, fkc_ref, o_ref,
                  qt_ref, fkb_ref, m_ref, l_ref, acc_ref, s4_ref, p2_ref, al_ref, *, tq, tk):
    qi = pl.program_id(1)
    ki = pl.program_id(2)
    npair = H_FOX // 2
    row_hi = lax.broadcasted_iota(jnp.int32, (LANES, tq), 0) >= DH
    nacc = 4

    @pl.when(ki == 0)
    def _():
        q = q_ref[...] * (DH ** -0.5)
        for p in range(npair):
            qt = q[:, p * LANES:(p + 1) * LANES].T
            qt_ref[2 * p] = jnp.where(row_hi, 0.0, qt).astype(BF16)
            qt_ref[2 * p + 1] = jnp.where(row_hi, qt, 0.0).astype(BF16)
        m_ref[...] = jnp.full(m_ref.shape, NEG_INF, F32)
        l_ref[...] = jnp.zeros_like(l_ref)
        acc_ref[...] = jnp.zeros_like(acc_ref)

    def tree(op, xs):
        while len(xs) > 1:
            xs = [op(xs[i], xs[i + 1]) for i in range(0, len(xs) - 1, 2)] + ([xs[-1]] if len(xs) % 2 else [])
        return xs[0]

    def step(masked):
        fkc = fkc_ref[...]
        for h in range(H_FOX):
            fkb_ref[h] = jnp.broadcast_to(fkc[:, h:h + 1], (tk, LANES))

        def scores(p, slot):
            for e in range(2):
                s4_ref[slot + e] = _dot(kb_ref[p], qt_ref[2 * p + e])

        scores(0, 0)

        def body(pp, carry):
            one_pair(2 * pp, 0)
            one_pair(2 * pp + 1, 2)
            return carry

        def one_pair(p, cur):
            vt = vt_ref[p]
            pvs = []
            scores(jnp.minimum(p + 1, npair - 1), 2 - cur)
            for e in range(2):
                h = 2 * p + e
                s_ref, p_ref = s4_ref.at[cur + e], p2_ref.at[e]
                fq_row = fqr_ref[pl.ds(h, 1), :]
                m_all, l_all = m_ref[h], l_ref[h]
                m_news, l_news, alphas, nrows = [], [], [], []
                for c in range(tq // LANES):
                    cs = slice(c * LANES, (c + 1) * LANES)
                    nr = min(tk, LANES * (c + 1)) if masked else tk
                    nrows.append(nr)
                    fq8 = jnp.broadcast_to(fq_row[:, cs], (SUBLANES, LANES))
                    mx = [jnp.full((SUBLANES, LANES), NEG_INF, F32)] * nacc
                    for r in range(nr // SUBLANES):
                        rs = slice(r * SUBLANES, (r + 1) * SUBLANES)
                        s = s_ref[rs, cs] + fq8 - fkb_ref[h, rs, :]
                        if masked and (r + 1) * SUBLANES > c * LANES:
                            key = lax.broadcasted_iota(jnp.int32, (SUBLANES, LANES), 0) + r * SUBLANES
                            qry = lax.broadcasted_iota(jnp.int32, (SUBLANES, LANES), 1) + c * LANES
                            s = jnp.where(key <= qry, s, NEG_INF)
                        s_ref[rs, cs] = s
                        mx[r % nacc] = jnp.maximum(mx[r % nacc], s)
                    m_prev = m_all[:, cs]
                    m_new = jnp.maximum(m_prev, jnp.max(tree(jnp.maximum, mx), axis=0, keepdims=True))
                    m_news.append(m_new)
                    alphas.append(jnp.exp(m_prev - m_new))
                for c in range(tq // LANES):
                    cs = slice(c * LANES, (c + 1) * LANES)
                    nr = nrows[c]
                    m16 = jnp.broadcast_to(m_news[c], (2 * SUBLANES, LANES))
                    sm = [jnp.zeros((2 * SUBLANES, LANES), F32)] * nacc
                    for r in range(nr // (2 * SUBLANES)):
                        rs = slice(r * 2 * SUBLANES, (r + 1) * 2 * SUBLANES)
                        pe = jnp.exp(s_ref[rs, cs] - m16)
                        sm[r % nacc] = sm[r % nacc] + pe
                        p_ref[rs, cs] = pe.astype(BF16)
                    if nr < tk:
                        p_ref[nr:tk, cs] = jnp.zeros((tk - nr, LANES), BF16)
                    l_news.append(alphas[c] * l_all[:, cs]
                                  + jnp.sum(tree(jnp.add, sm), axis=0, keepdims=True))
                m_ref[h] = jnp.concatenate(m_news, axis=1)
                l_ref[h] = jnp.concatenate(l_news, axis=1)
                al_ref[e] = jnp.concatenate(alphas, axis=1)
                pvs.append(_dot(vt, p_ref[...]))
            acc_ref[p] = (jnp.where(row_hi, al_ref[1], al_ref[0]) * acc_ref[p]
                          + jnp.where(row_hi, pvs[1], pvs[0]))

        lax.fori_loop(0, npair // 2, body, 0)

    @pl.when(ki < qi)
    def _():
        step(False)

    @pl.when(ki == qi)
    def _():
        step(True)
        for p in range(npair):
            denom = jnp.where(row_hi, l_ref[2 * p + 1], l_ref[2 * p])
            o_ref[:, p * LANES:(p + 1) * LANES] = (acc_ref[p] / denom).T.astype(o_ref.dtype)


def fox_prompt(z, kb, vbt, f_t, f_col, *, seq_len):
    t = z.shape[0]
    nb = t // seq_len
    tq = tk = _pick(seq_len, (512, 256, 128))
    nq = seq_len // tq
    npair = H_FOX // 2
    return pl.pallas_call(
        functools.partial(_fox_p_kernel, tq=tq, tk=tk),
        grid=(nb, nq, nq),
        in_specs=[
            pl.BlockSpec((tq, D_FOX), lambda b, i, j: (b * nq + i, 0)),
            pl.BlockSpec((npair, tk, LANES), lambda b, i, j: (0, b * nq + jnp.minimum(i, j), 0)),
            pl.BlockSpec((None, npair, LANES, tk), lambda b, i, j: (b, 0, 0, jnp.minimum(i, j))),
            pl.BlockSpec((None, HEAD_PAD, tq), lambda b, i, j: (b, 0, i)),
            pl.BlockSpec((tk, HEAD_PAD), lambda b, i, j: (b * nq + jnp.minimum(i, j), 0)),
        ],
        out_specs=pl.BlockSpec((tq, D_FOX), lambda b, i, j: (b * nq + i, 0)),
        out_shape=jax.ShapeDtypeStruct((t, D_FOX), BF16),
        scratch_shapes=[
            pltpu.VMEM((H_FOX, LANES, tq), BF16),
            pltpu.VMEM((H_FOX, tk, LANES), F32),
            pltpu.VMEM((H_FOX, 1, tq), F32),
            pltpu.VMEM((H_FOX, 1, tq), F32),
            pltpu.VMEM((npair, LANES, tq), F32),
            pltpu.VMEM((4, tk, tq), F32),
            pltpu.VMEM((2, tk, tq), BF16),
            pltpu.VMEM((2, 1, tq), F32),
        ],
        compiler_params=_params(("arbitrary", "arbitrary", "arbitrary")),
        name="fox_prompt",
    )(z, kb, vbt, f_t, f_col)


def _fox_s_kernel(pt_ref, q_ref, kn_ref, vn_ref, lfr_ref, lfn_ref, *rest, npg, nj, ls):
    lf_refs = rest[:npg]
    k_refs = rest[npg:2 * npg]
    v_refs = rest[2 * npg:3 * npg]
    o_ref, qbd_ref, fq_ref, carry_ref, m_ref, l_ref, acc_ref = rest[3 * npg:]
    j = pl.program_id(1)
    nrow = H_FOX * ls

    def update(s_raw, fk, mask, vs, v_transposed):
        fq = fq_ref[...]
        s = jnp.concatenate(
            [s_raw[h * ls:(h + 1) * ls, :] + fq[h * ls:(h + 1) * ls, :] - fk[h:h + 1, :]
             for h in range(H_FOX)], axis=0)
        if mask is not None:
            s = jnp.where(mask, s, NEG_INF)
        m_prev = m_ref[...]
        m_new = jnp.maximum(m_prev, jnp.max(s, axis=-1, keepdims=True))
        alpha = jnp.exp(m_prev - m_new)
        pe = jnp.exp(s - m_new)
        l_ref[...] = alpha * l_ref[...] + jnp.sum(pe, axis=-1, keepdims=True)
        m_ref[...] = m_new
        pb = pe.astype(BF16)
        mm = _dot_nt if v_transposed else _dot
        pv = mm(pb[:, :LANES], vs[0])
        for u in range(1, len(vs)):
            pv = pv + mm(pb[:, u * LANES:(u + 1) * LANES], vs[u])
        acc_ref[...] = alpha * acc_ref[...] + pv

    @pl.when(j == 0)
    def _():
        q = q_ref[...] * (DH ** -0.5)
        head = _lane(q.shape) >> 6
        qbd_ref[...] = jnp.concatenate(
            [jnp.where(head == h, q, 0.0) for h in range(H_FOX)], axis=0).astype(BF16)
        m_ref[...] = jnp.full(m_ref.shape, NEG_INF, F32)
        l_ref[...] = jnp.zeros_like(l_ref)
        acc_ref[...] = jnp.zeros_like(acc_ref)
        lfr = lfr_ref[...]
        fq_rows = -(_suffix_scan_rows(lfr) - lfr)
        fq_ref[...] = jnp.concatenate([fq_rows[:, h:h + 1] for h in range(H_FOX)], axis=0)
        lfn = lfn_ref[...]
        inc = _suffix_scan(lfn)
        carry_ref[...] = jnp.broadcast_to(inc[:, 0:1], carry_ref.shape)
        kn = _pad_rows(kn_ref[...], LANES).astype(BF16)
        vn = _pad_rows(vn_ref[...], LANES).astype(BF16)
        col = _lane((nrow, LANES))
        qrow = lax.broadcasted_iota(jnp.int32, (nrow, LANES), 0) & (ls - 1)
        update(_dot_nt(qbd_ref[...], kn), -(inc - lfn), col <= qrow, [vn], False)

    carry = carry_ref[...]
    fks = []
    for u in range(npg):
        x = lf_refs[u][...]
        inc = _suffix_scan(x)
        fks.append(-(carry + (inc - x)))
        carry = carry + inc[:, 0:1]
    carry_ref[...] = carry
    qbd = qbd_ref[...]
    s_raw = jnp.concatenate([_dot(qbd, k_refs[u][...].astype(BF16)) for u in range(npg)], axis=1)
    update(s_raw, jnp.concatenate(fks, axis=1), None,
           [v_refs[u][...].astype(BF16) for u in range(npg)], True)

    @pl.when(j == nj - 1)
    def _():
        acc = acc_ref[...] / l_ref[...]
        head = _lane((ls, D_FOX)) >> 6
        out = jnp.zeros((ls, D_FOX), F32)
        for h in range(H_FOX):
            out = out + jnp.where(head == h, acc[h * ls:(h + 1) * ls, :], 0.0)
        o_ref[...] = out


def fox_sample(z, k_new, v_new, lf_rows, lf_new_t, cache_lf_t, cache_kt, cache_vt, page_table, *, seq_len):
    t = z.shape[0]
    ls = seq_len
    nb = t // ls
    n_pages = page_table.shape[1]
    npg = 8
    nj = n_pages // npg
    page = cache_kt.shape[2]
    nrow = H_FOX * ls

    def page_spec(u, rows):
        return pl.BlockSpec((None, rows, page),
                            lambda b, j, pt: (pt[b, n_pages - 1 - (j * npg + u)], 0, 0))

    row_spec = pl.BlockSpec((ls, D_FOX), lambda b, j, pt: (b, 0))
    return pl.pallas_call(
        functools.partial(_fox_s_kernel, npg=npg, nj=nj, ls=ls),
        grid_spec=pltpu.PrefetchScalarGridSpec(
            num_scalar_prefetch=1,
            grid=(nb, nj),
            in_specs=[
                row_spec, row_spec, row_spec,
                pl.BlockSpec((ls, LANES), lambda b, j, pt: (b, 0)),
                pl.BlockSpec((None, HEAD_PAD, LANES), lambda b, j, pt: (b, 0, 0)),
            ] + [page_spec(u, HEAD_PAD) for u in range(npg)]
              + [page_spec(u, D_FOX) for u in range(npg)] * 2,
            out_specs=row_spec,
            scratch_shapes=[
                pltpu.VMEM((nrow, D_FOX), BF16),
                pltpu.VMEM((nrow, 1), F32),
                pltpu.VMEM((HEAD_PAD, LANES), F32),
                pltpu.VMEM((nrow, 1), F32),
                pltpu.VMEM((nrow, 1), F32),
                pltpu.VMEM((nrow, D_FOX), F32),
            ]),
        out_shape=jax.ShapeDtypeStruct((t, D_FOX), F32),
        compiler_params=_params(("arbitrary", "arbitrary")),
        name="fox_sample",
    )(page_table, z, k_new, v_new, lf_rows, lf_new_t,
      *([cache_lf_t] * npg), *([cache_kt] * npg), *([cache_vt] * npg))


def _heads_to_sublanes(logf, nb, seq_len):
    x = logf[:, :H_FOX].reshape(nb, seq_len, H_FOX).transpose(0, 2, 1)
    return jnp.pad(x, ((0, 0), (0, HEAD_PAD - H_FOX), (0, 0)))


def _trunk(x, pos, mem_k, mem_v, ret_state, conv_state, past, p):
    nb, seq_len, d = x.shape
    t = nb * seq_len
    sample = past is not None
    cat_dtype = F32 if sample else BF16
    h = x.reshape(t, d)
    new_ret, new_conv = [], []
    k_sh = v_sh = logf_sh = None
    for layer in range(DEPTH):
        if layer < N_A:
            z = norm_matmul(h, p['g_mix_pre'][layer], p['w_in_a'], layer)
            s0 = (jnp.zeros((nb, H_RET, DK_RET, DV_RET), F32) if ret_state is None else ret_state[layer])
            r, s_new = retention(z, s0, pos, seq_len=seq_len, out_dtype=cat_dtype)
            new_ret.append(s_new)
            m = mem_attend(z, (2 * D_RET_QK + 2 * D_RET_V) // D_MEMQ, mem_k[layer], mem_v[layer],
                           seq_len=seq_len, out_dtype=cat_dtype)
            h = out_proj_residual(r, m, p['w_out_a'], layer, p['g_mix_post'][layer], h)
        else:
            if layer == N_A:
                k_sh, v_sh, lf, *attn = shared_kv(h, p['g_kv'], p['wk'], p['wv'], p['wf'], p['bf'],
                                                  seq_len=seq_len, attn_copies=not sample)
                logf_sh = lf[:, :H_FOX]
                lf_t = _heads_to_sublanes(lf, nb, seq_len)
                if not sample:
                    kb, vbt = attn
                    f_t = forget_suffix(lf_t)
                    f_col = f_t.transpose(0, 2, 1).reshape(t, HEAD_PAD)
                else:
                    cache_kt, cache_vt, cache_lf_t, page_table = past
                    lf_new_t = jnp.pad(lf_t, ((0, 0), (0, 0), (0, LANES - seq_len)))
            j = layer - N_A
            z = norm_matmul(h, p['g_mix_pre'][layer], p['w_in_b'], j)
            if not sample:
                o = fox_prompt(z, kb, vbt, f_t, f_col, seq_len=seq_len)
            else:
                o = fox_sample(z, k_sh, v_sh, lf, lf_new_t, cache_lf_t, cache_kt, cache_vt, page_table,
                               seq_len=seq_len)
            m = mem_attend(z, D_FOX // D_MEMQ, mem_k[layer], mem_v[layer],
                           seq_len=seq_len, out_dtype=cat_dtype)
            h = out_proj_residual(o, m, p['w_out_b'], j, p['g_mix_post'][layer], h)
        h, cs = conv_ffn(h, layer, p, seq_len=seq_len,
                         conv_state=None if conv_state is None else conv_state[layer])
        new_conv.append(cs)
    y = h.reshape(nb, seq_len, d)
    k_o = k_sh.reshape(nb, seq_len, H_FOX, DH)
    v_o = v_sh.reshape(nb, seq_len, H_FOX, DH)
    logf_o = logf_sh.reshape(nb, seq_len, H_FOX)
    return y, k_o, v_o, logf_o, jnp.stack(new_ret), jnp.stack(new_conv)


def kernel(x_prompt, x_sample, mem_prompt, cache_k, cache_v, cache_logf, page_table, cache_mem_k, cache_mem_v, state_ret, state_conv, g_mix_pre, g_mix_post, g_ffn_pre, g_ffn_post, w_in_a, w_out_a, w_in_b, w_out_b, w_mem_kv, g_kv, w_kv_shared, b_f, w_ffn_up, conv_w, conv_b, w_ffn_down):
    wf = jnp.pad(w_kv_shared[:, 2 * D_FOX:], ((0, 0), (0, LANES - H_FOX))).astype(BF16)
    bf = jnp.pad(b_f, (0, LANES - H_FOX)).reshape(1, LANES)
    p = {'g_mix_pre': g_mix_pre, 'g_mix_post': g_mix_post, 'g_ffn_pre': g_ffn_pre, 'g_ffn_post': g_ffn_post,
         'w_in_a': w_in_a.astype(BF16), 'w_out_a': w_out_a.astype(BF16),
         'w_in_b': w_in_b.astype(BF16), 'w_out_b': w_out_b.astype(BF16),
         'g_kv': g_kv, 'wk': w_kv_shared[:, :D_FOX].astype(BF16),
         'wv': w_kv_shared[:, D_FOX:2 * D_FOX].astype(BF16), 'wf': wf, 'bf': bf,
         'w_ffn_up': w_ffn_up.astype(BF16), 'conv_w': conv_w, 'conv_b': conv_b,
         'w_ffn_down': w_ffn_down.astype(BF16)}

    nb, seq_len, d = x_prompt.shape
    n_mem = mem_prompt.shape[1]
    mk, mv = mem_kv(mem_prompt.reshape(nb * n_mem, d), w_mem_kv.astype(BF16))
    mk = mk.reshape(DEPTH, nb, n_mem, D_MEMQ)
    mv = mv.reshape(DEPTH, nb, n_mem, D_MEMQ)
    pos_p = jnp.arange(seq_len, dtype=jnp.int32)
    y_p, k_p, v_p, logf_p, ret_p, conv_p = _trunk(x_prompt, pos_p, mk, mv, None, None, None, p)
    mem_k_p = mk.reshape(DEPTH, nb, n_mem, H_MEM, DH)
    mem_v_p = mv.reshape(DEPTH, nb, n_mem, H_MEM, DH)

    db, ls, _ = x_sample.shape
    n_phys, page = cache_k.shape[:2]
    past_len = page_table.shape[1] * page
    pos_s = past_len + jnp.arange(ls, dtype=jnp.int32)
    cache_lf_t = jnp.pad(cache_logf.transpose(0, 2, 1), ((0, 0), (0, HEAD_PAD - H_FOX), (0, 0)))
    cache_kt = cache_k.transpose(0, 2, 3, 1).reshape(n_phys, D_FOX, page)
    cache_vt = cache_v.transpose(0, 2, 3, 1).reshape(n_phys, D_FOX, page)
    past = (cache_kt, cache_vt, cache_lf_t, page_table)
    y_s, k_s, v_s, logf_s, ret_s, conv_s = _trunk(
        x_sample, pos_s, cache_mem_k.reshape(DEPTH, db, N_MEM, D_MEMQ),
        cache_mem_v.reshape(DEPTH, db, N_MEM, D_MEMQ), state_ret, state_conv, past, p)

    return (y_p, y_s, k_p, v_p, logf_p, k_s, v_s, logf_s,
            mem_k_p, mem_v_p, ret_p, ret_s, conv_p, conv_s)
```

```python
import functools

import jax
import jax.numpy as jnp
from jax import lax
from jax.experimental import pallas as pl
from jax.experimental.pallas import tpu as pltpu

F32 = jnp.float32
BF16 = jnp.bfloat16

D_MODEL = 1024
DEPTH = 4
N_A = 2
H_RET = 6
DK_RET = 64
DV_RET = 128
H_FOX = 12
DH = 64
H_MEM = 4
N_MEM = 256
D_RET_QK = H_RET * DK_RET
D_RET_V = H_RET * DV_RET
D_FOX = H_FOX * DH
D_MEMQ = H_MEM * DH
D_FF = 2816
RET_CHUNK = 128
ROPE_BASE = 10000.0
EPS = 1e-6

LANES = 128
HEAD_PAD = 16
VMEM_LIMIT = 48 * 1024 * 1024

NEG_INF = float("-inf")


def _params(sem):
    return pltpu.CompilerParams(dimension_semantics=sem, vmem_limit_bytes=VMEM_LIMIT)


def _rms(x, g):
    return x * lax.rsqrt(jnp.mean(x * x, axis=-1, keepdims=True) + EPS) * g


def _dot(a, b):
    return jnp.dot(a, b, preferred_element_type=F32)


def _dot_nt(a, b):
    return lax.dot_general(a, b, (((1,), (1,)), ((), ())), preferred_element_type=F32)


def _lane(shape):
    return lax.broadcasted_iota(jnp.int32, shape, len(shape) - 1)


def _pad_rows(x, n):
    if x.shape[0] == n:
        return x
    return jnp.concatenate([x, jnp.zeros((n - x.shape[0],) + x.shape[1:], x.dtype)], axis=0)


def _pick(n, cands):
    for c in cands:
        if n % c == 0:
            return c
    return n


def _nm_kernel(x_ref, g_ref, w_ref, o_ref, xn_ref, *, norm):
    @pl.when(pl.program_id(1) == 0)
    def _():
        x = x_ref[...]
        if norm:
            x = _rms(x, g_ref[...])
        xn_ref[...] = x.astype(BF16)

    o_ref[...] = _dot(xn_ref[...], w_ref[...])


def norm_matmul(x, g, w, layer, *, norm=True):
    t, d = x.shape
    n = w.shape[-1]
    tm = _pick(t, (1024, 512, 256))
    tn = _pick(n, (512, 256, 128))
    return pl.pallas_call(
        functools.partial(_nm_kernel, norm=norm),
        grid=(t // tm, n // tn),
        in_specs=[
            pl.BlockSpec((tm, d), lambda i, j: (i, 0)),
            pl.BlockSpec((1, d), lambda i, j: (0, 0)),
            pl.BlockSpec((None, d, tn), lambda i, j: (layer, 0, j)),
        ],
        out_specs=pl.BlockSpec((tm, tn), lambda i, j: (i, j)),
        out_shape=jax.ShapeDtypeStruct((t, n), F32),
        scratch_shapes=[pltpu.VMEM((tm, d), BF16)],
        compiler_params=_params(("arbitrary", "arbitrary")),
        name="norm_matmul",
    )(x, g.reshape(1, d), w)


def _memkv_kernel(x_ref, w_ref, k_ref, v_ref):
    acc = _dot(x_ref[...].astype(BF16), w_ref[...])
    k_ref[...] = acc[:, :D_MEMQ]
    v_ref[...] = acc[:, D_MEMQ:]


def mem_kv(x, w):
    t, d = x.shape
    tm = _pick(t, (1024, 512, 256))
    out = jax.ShapeDtypeStruct((DEPTH, t, D_MEMQ), F32)
    return pl.pallas_call(
        _memkv_kernel,
        grid=(t // tm, DEPTH),
        in_specs=[
            pl.BlockSpec((tm, d), lambda i, l: (i, 0)),
            pl.BlockSpec((None, d, 2 * D_MEMQ), lambda i, l: (l, 0, 0)),
        ],
        out_specs=[pl.BlockSpec((None, tm, D_MEMQ), lambda i, l: (l, i, 0))] * 2,
        out_shape=[out, out],
        compiler_params=_params(("arbitrary", "arbitrary")),
        name="mem_kv",
    )(x, w)


def _log_sigmoid(x):
    return jnp.minimum(x, 0.0) - jnp.log1p(jnp.exp(-jnp.abs(x)))


def _kv_kernel(x_ref, g_ref, wk_ref, wv_ref, wf_ref, bf_ref, k_ref, v_ref, lf_ref, *attn_refs):
    xn = _rms(x_ref[...], g_ref[...]).astype(BF16)
    k = _dot(xn, wk_ref[...])
    v = _dot(xn, wv_ref[...])
    k_ref[...] = k
    v_ref[...] = v
    lf_ref[...] = _log_sigmoid(_dot(xn, wf_ref[...]) + bf_ref[...])
    if attn_refs:
        kb_ref, vbt_ref = attn_refs
        for p in range(D_FOX // LANES):
            kb_ref[p] = k[:, p * LANES:(p + 1) * LANES].astype(BF16)
            vbt_ref[p] = v[:, p * LANES:(p + 1) * LANES].T.astype(BF16)


def shared_kv(h, g, wk, wv, wf, bf, *, seq_len, attn_copies):
    t, d = h.shape
    tm = _pick(t, (512, 256))
    npair = D_FOX // LANES
    full = lambda shape: pl.BlockSpec(shape, lambda i: (0,) * len(shape))
    out_specs = [
        pl.BlockSpec((tm, D_FOX), lambda i: (i, 0)),
        pl.BlockSpec((tm, D_FOX), lambda i: (i, 0)),
        pl.BlockSpec((tm, LANES), lambda i: (i, 0)),
    ]
    out_shape = [
        jax.ShapeDtypeStruct((t, D_FOX), F32),
        jax.ShapeDtypeStruct((t, D_FOX), F32),
        jax.ShapeDtypeStruct((t, LANES), F32),
    ]
    if attn_copies:
        tpb = seq_len // tm
        out_specs += [
            pl.BlockSpec((npair, tm, LANES), lambda i: (0, i, 0)),
            pl.BlockSpec((None, npair, LANES, tm), lambda i: (i // tpb, 0, 0, i % tpb)),
        ]
        out_shape += [
            jax.ShapeDtypeStruct((npair, t, LANES), BF16),
            jax.ShapeDtypeStruct((t // seq_len, npair, LANES, seq_len), BF16),
        ]
    return pl.pallas_call(
        _kv_kernel,
        grid=(t // tm,),
        in_specs=[
            pl.BlockSpec((tm, d), lambda i: (i, 0)),
            full((1, d)), full((d, D_FOX)), full((d, D_FOX)), full((d, LANES)), full((1, LANES)),
        ],
        out_specs=out_specs,
        out_shape=out_shape,
        compiler_params=_params(("arbitrary",)),
        name="shared_kv",
    )(h, g.reshape(1, d), wk, wv, wf, bf)


def _mnr_kernel(c1_ref, c2_ref, w_ref, g_ref, h_ref, o_ref):
    n1 = c1_ref.shape[1]
    o = (_dot(c1_ref[...].astype(BF16), w_ref[:n1, :])
         + _dot(c2_ref[...].astype(BF16), w_ref[n1:, :]))
    o_ref[...] = h_ref[...] + _rms(o, g_ref[...])


def out_proj_residual(c1, c2, w, layer, g, h):
    t, d = h.shape
    n1, n2 = c1.shape[1], c2.shape[1]
    tm = _pick(t, (512, 256))
    return pl.pallas_call(
        _mnr_kernel,
        grid=(t // tm,),
        in_specs=[
            pl.BlockSpec((tm, n1), lambda i: (i, 0)),
            pl.BlockSpec((tm, n2), lambda i: (i, 0)),
            pl.BlockSpec((None, n1 + n2, d), lambda i: (layer, 0, 0)),
            pl.BlockSpec((1, d), lambda i: (0, 0)),
            pl.BlockSpec((tm, d), lambda i: (i, 0)),
        ],
        out_specs=pl.BlockSpec((tm, d), lambda i: (i, 0)),
        out_shape=jax.ShapeDtypeStruct((t, d), F32),
        compiler_params=_params(("arbitrary",)),
        name="out_proj_residual",
    )(c1, c2, w, g.reshape(1, d), h)


FFN_COLS = 1408
FFN_ROWS = 16
FFN_CHUNK = 128


def _gelu_tanh(x):
    return 0.5 * x * (1.0 + jnp.tanh(0.7978845608028654 * (x + 0.044715 * (x * x * x))))


def _ffn_kernel(*refs, tm, tf, nf, tiles_per_batch, seq_rows):
    (h_ref, gpre_ref, wg_ref, wv_ref, cwg_ref, cwv_ref, cbg_ref, cbv_ref, wd_ref, gpost_ref) = refs[:10]
    if seq_rows is None:
        o_ref, tg_ref, tv_ref, xn_ref, acc_ref, ug_ref, uv_ref, hid_ref, cg_ref, cv_ref = refs[10:]
        p1g_ref = p1v_ref = p2g_ref = p2v_ref = None
    else:
        (p1g_ref, p1v_ref, p2g_ref, p2v_ref, o_ref, tg_ref, tv_ref,
         xn_ref, acc_ref, ug_ref, uv_ref, hid_ref) = refs[10:]
        cg_ref = cv_ref = None
    i = pl.program_id(0)
    f = pl.program_id(1)
    rg = FFN_ROWS

    ch = FFN_CHUNK

    @pl.when(f == 0)
    def _():
        xn_ref[...] = _rms(h_ref[...], gpre_ref[...]).astype(BF16)
        acc_ref[...] = jnp.zeros_like(acc_ref)

    for u_ref, c_ref in ((ug_ref, cg_ref), (uv_ref, cv_ref)):
        if seq_rows is None:
            @pl.when((i % tiles_per_batch) == 0)
            def _():
                c_ref[f] = jnp.zeros((8, tf), F32)

            u_ref[0:8, :] = c_ref[f]
        else:
            u_ref[0:8, :] = jnp.zeros((8, tf), F32)

    if seq_rows is not None:
        r = lax.broadcasted_iota(jnp.int32, (rg, LANES), 0) & (seq_rows - 1)
        keep1 = r >= 1
        keep2 = r >= 2

    def up(c):
        xc = xn_ref[c * ch:(c + 1) * ch, :]
        ug_ref[8 + c * ch:8 + (c + 1) * ch, :] = _dot(xc, wg_ref[...])
        uv_ref[8 + c * ch:8 + (c + 1) * ch, :] = _dot(xc, wv_ref[...])

    def conv(u_ref, cw_ref, cb_ref, p1_ref, p2_ref, r0, sl):
        a = u_ref[r0:r0 + rg + 8, sl]
        u = a[8:]
        u1 = pltpu.roll(a, 1, 0)[8:]
        u2 = pltpu.roll(a, 2, 0)[8:]
        if seq_rows is not None:
            u1 = jnp.where(keep1, u1, p1_ref[r0:r0 + rg, sl])
            u2 = jnp.where(keep2, u2, p2_ref[r0:r0 + rg, sl])
        return cb_ref[:, sl] + cw_ref[0:1, sl] * u2 + cw_ref[1:2, sl] * u1 + cw_ref[2:3, sl] * u

    def gate_mul(c):
        for lt in range(tf // LANES):
            sl = slice(lt * LANES, (lt + 1) * LANES)
            for r0 in range(c * ch, (c + 1) * ch, rg):
                gate = conv(ug_ref, cwg_ref, cbg_ref, p1g_ref, p2g_ref, r0, sl)
                val = conv(uv_ref, cwv_ref, cbv_ref, p1v_ref, p2v_ref, r0, sl)
                hid_ref[r0:r0 + rg, sl] = (_gelu_tanh(gate) * val).astype(BF16)

    def down(c):
        rows = slice(c * ch, (c + 1) * ch)
        acc_ref[rows, :] += _dot(hid_ref[rows, :], wd_ref[...])

    nch = tm // ch
    ahead = 2
    for c in range(min(ahead, nch)):
        up(c)
    for c in range(nch):
        if c + ahead < nch:
            up(c + ahead)
        gate_mul(c)
        down(c)

    if seq_rows is None:
        cg_ref[f] = ug_ref[tm:tm + 8, :]
        cv_ref[f] = uv_ref[tm:tm + 8, :]
        tg_ref[0] = ug_ref[tm:tm + 8, :]
        tv_ref[0] = uv_ref[tm:tm + 8, :]
    else:
        tg_ref[...] = ug_ref[8:tm + 8, :]
        tv_ref[...] = uv_ref[8:tm + 8, :]

    @pl.when(f == nf - 1)
    def _():
        o_ref[...] = h_ref[...] + _rms(acc_ref[...], gpost_ref[...])


def conv_ffn(h, layer, p, *, seq_len, conv_state=None):
    t, d = h.shape
    nb = t // seq_len
    tf = FFN_COLS
    nf = D_FF // tf
    w_up, cw, cb, w_down = p['w_ffn_up'], p['conv_w'], p['conv_b'], p['w_ffn_down']
    cb3 = cb.reshape(DEPTH, 1, 2 * D_FF)
    if conv_state is None:
        tm = _pick(seq_len, (512, 256))
        tiles_per_batch = seq_len // tm
        seq_rows = None
        tail_shape = jax.ShapeDtypeStruct((t // tm, 8, D_FF), F32)
        tail_spec = pl.BlockSpec((1, 8, tf), lambda i, f: (i, 0, f))
        extra_in, extra_specs = [], []
        extra_scratch = [pltpu.VMEM((nf, 8, tf), F32), pltpu.VMEM((nf, 8, tf), F32)]
    else:
        tm = t
        tiles_per_batch = 1
        seq_rows = seq_len
        tail_shape = jax.ShapeDtypeStruct((t, D_FF), F32)
        tail_spec = pl.BlockSpec((tm, tf), lambda i, f: (i, f))
        st = conv_state
        z = jnp.zeros((nb, seq_len, 2 * D_FF), F32)
        prev1 = z.at[:, 0].set(st[:, 1]).reshape(t, 2 * D_FF)
        prev2 = z.at[:, 0].set(st[:, 0]).at[:, 1].set(st[:, 1]).reshape(t, 2 * D_FF)
        extra_in = [prev1, prev1, prev2, prev2]
        gspec = pl.BlockSpec((tm, tf), lambda i, f: (i, f))
        vspec = pl.BlockSpec((tm, tf), lambda i, f: (i, nf + f))
        extra_specs = [gspec, vspec, gspec, vspec]
        extra_scratch = []
    kern = functools.partial(_ffn_kernel, tm=tm, tf=tf, nf=nf, tiles_per_batch=tiles_per_batch,
                             seq_rows=seq_rows)
    h_new, tail_g, tail_v = pl.pallas_call(
        kern,
        grid=(t // tm, nf),
        in_specs=[
            pl.BlockSpec((tm, d), lambda i, f: (i, 0)),
            pl.BlockSpec((1, d), lambda i, f: (0, 0)),
            pl.BlockSpec((None, d, tf), lambda i, f: (layer, 0, f)),
            pl.BlockSpec((None, d, tf), lambda i, f: (layer, 0, nf + f)),
            pl.BlockSpec((None, 3, tf), lambda i, f: (layer, 0, f)),
            pl.BlockSpec((None, 3, tf), lambda i, f: (layer, 0, nf + f)),
            pl.BlockSpec((None, 1, tf), lambda i, f: (layer, 0, f)),
            pl.BlockSpec((None, 1, tf), lambda i, f: (layer, 0, nf + f)),
            pl.BlockSpec((None, tf, d), lambda i, f: (layer, f, 0)),
            pl.BlockSpec((1, d), lambda i, f: (0, 0)),
        ] + extra_specs,
        out_specs=[pl.BlockSpec((tm, d), lambda i, f: (i, 0)), tail_spec, tail_spec],
        out_shape=[jax.ShapeDtypeStruct((t, d), F32), tail_shape, tail_shape],
        scratch_shapes=[pltpu.VMEM((tm, d), BF16), pltpu.VMEM((tm, d), F32),
                        pltpu.VMEM((tm + 8, tf), F32), pltpu.VMEM((tm + 8, tf), F32),
                        pltpu.VMEM((tm, tf), BF16)] + extra_scratch,
        compiler_params=_params(("arbitrary", "arbitrary")),
        name="conv_ffn",
    )(h, p['g_ffn_pre'][layer].reshape(1, d), w_up, w_up, cw, cw, cb3, cb3, w_down,
      p['g_ffn_post'][layer].reshape(1, d), *extra_in)
    if conv_state is None:
        last = slice(tiles_per_batch - 1, None, tiles_per_batch)
        new_state = jnp.concatenate([tail_g[last, 6:8], tail_v[last, 6:8]], axis=-1)
    else:
        new_state = jnp.concatenate([tail_g.reshape(nb, seq_len, D_FF)[:, seq_len - 2:],
                                     tail_v.reshape(nb, seq_len, D_FF)[:, seq_len - 2:]], axis=-1)
    return h_new, new_state


def _rotary(x, cos, sins):
    parts = []
    for p in range(x.shape[1] // LANES):
        xb = x[:, p * LANES:(p + 1) * LANES]
        first = (_lane(xb.shape) & (DK_RET // 2)) == 0
        sw = jnp.where(first, pltpu.roll(xb, LANES - DK_RET // 2, 1), pltpu.roll(xb, DK_RET // 2, 1))
        parts.append(xb * cos[:, p * LANES:(p + 1) * LANES] + sw * sins[:, p * LANES:(p + 1) * LANES])
    return jnp.concatenate(parts, axis=1)


def _ret_kernel(q_ref, k_ref, v_ref, g_ref, cos_ref, sin_ref, dm_ref, qd_ref, kd_ref, sd_ref, s0_ref,
                r_ref, so_ref, s_ref, *, tq, nq):
    c = RET_CHUNK
    qi = pl.program_id(1)
    zero_half = jnp.zeros((DK_RET, DV_RET), F32)

    @pl.when(qi == 0)
    def _():
        for h in range(H_RET):
            halves = [zero_half, zero_half]
            halves[h % 2] = s0_ref[h]
            s_ref[h] = jnp.concatenate(halves, axis=0)

    cos = cos_ref[...]
    sins = sin_ref[...]
    q = _rotary(q_ref[...], cos, sins)
    k = _rotary(k_ref[...], cos, sins) * (DK_RET ** -0.5)
    rows = max(tq, c)
    for cs in range(0, rows, c):
        n = min(c, tq)
        qc = _pad_rows(q[cs:cs + n], c)
        kc = _pad_rows(k[cs:cs + n], c)
        vc = _pad_rows(v_ref[cs:cs + n, :], c)
        qd = qc * qd_ref[...]
        kd = kc * kd_ref[...]
        for p in range(H_RET // 2):
            sl = slice(p * LANES, (p + 1) * LANES)
            q2, k2, qd2, kd2 = qc[:, sl], kc[:, sl], qd[:, sl], kd[:, sl]
            k2b = k2.astype(BF16)
            hi = _lane(q2.shape) >= DK_RET
            for e in range(2):
                h = 2 * p + e
                mine = hi if e else jnp.logical_not(hi)
                qm = jnp.where(mine, q2, 0.0).astype(BF16)
                qdm = jnp.where(mine, qd2, 0.0).astype(BF16)
                kdm = jnp.where(mine, kd2, 0.0)
                vh = vc[:, h * DV_RET:(h + 1) * DV_RET].astype(BF16)
                inner = (_dot_nt(qm, k2b) * dm_ref[h]).astype(BF16)
                s_old = s_ref[h]
                o = _dot(inner, vh) + _dot(qdm, s_old.astype(BF16))
                s_ref[h] = sd_ref[h] * s_old + _dot(kdm.T.astype(BF16), vh)
                cen = o - jnp.mean(o, axis=-1, keepdims=True)
                y = cen * lax.rsqrt(jnp.mean(cen * cen, axis=-1, keepdims=True) + EPS)
                gh = g_ref[cs:cs + n, h * DV_RET:(h + 1) * DV_RET]
                r = (gh / (1.0 + jnp.exp(-gh))) * y[:n]
                r_ref[cs:cs + n, h * DV_RET:(h + 1) * DV_RET] = r.astype(r_ref.dtype)

    @pl.when(qi == nq - 1)
    def _():
        for h in range(H_RET):
            e = h % 2
            so_ref[h] = s_ref[h][e * DK_RET:(e + 1) * DK_RET, :]


def _ret_tables(chunk):
    c = RET_CHUNK
    log_gamma = jnp.log1p(-(2.0 ** (-5.0 - jnp.arange(H_RET, dtype=F32))))
    idx = jnp.arange(c, dtype=F32)
    rel = idx[:, None] - idx[None, :]
    dmask = jnp.where(rel >= 0, jnp.exp(log_gamma[:, None, None] * jnp.maximum(rel, 0.0)), 0.0)
    qdec = jnp.exp(log_gamma[None, :] * (idx[:, None] + 1.0))
    kdec = jnp.exp(log_gamma[None, :] * (chunk - 1.0 - idx[:, None]))
    sdec = jnp.exp(log_gamma * chunk)
    qdec = jnp.repeat(qdec, DK_RET, axis=1)
    kdec = jnp.repeat(kdec, DK_RET, axis=1)
    sdec = jnp.broadcast_to(sdec[:, None, None], (H_RET, 1, DV_RET))
    return dmask, qdec, kdec, sdec


def _rope_tables(pos):
    half = DK_RET // 2
    inv = ROPE_BASE ** (-jnp.arange(half, dtype=F32) / half)
    ang = pos.astype(F32)[:, None] * inv[None, :]
    cos, sin = jnp.cos(ang), jnp.sin(ang)
    cos_t = jnp.tile(jnp.concatenate([cos, cos], axis=1), (1, H_RET))
    sin_t = jnp.tile(jnp.concatenate([-sin, sin], axis=1), (1, H_RET))
    return cos_t, sin_t


def retention(z, s0, pos, *, seq_len, out_dtype):
    t = z.shape[0]
    nb = t // seq_len
    chunk = RET_CHUNK if seq_len % RET_CHUNK == 0 else seq_len
    tq = _pick(seq_len, (512, 256, 128))
    nq = seq_len // tq
    dmask, qdec, kdec, sdec = _ret_tables(chunk)
    cos_t, sin_t = _rope_tables(pos)
    nqk = D_RET_QK
    full = lambda shape: pl.BlockSpec(shape, lambda b, i: (0,) * len(shape))
    return pl.pallas_call(
        functools.partial(_ret_kernel, tq=tq, nq=nq),
        grid=(nb, nq),
        in_specs=[
            pl.BlockSpec((tq, nqk), lambda b, i: (b * nq + i, 0)),
            pl.BlockSpec((tq, nqk), lambda b, i: (b * nq + i, 1)),
            pl.BlockSpec((tq, D_RET_V), lambda b, i: (b * nq + i, 1)),
            pl.BlockSpec((tq, D_RET_V), lambda b, i: (b * nq + i, 2)),
            pl.BlockSpec((tq, nqk), lambda b, i: (i, 0)),
            pl.BlockSpec((tq, nqk), lambda b, i: (i, 0)),
            full((H_RET, RET_CHUNK, RET_CHUNK)),
            full((RET_CHUNK, nqk)),
            full((RET_CHUNK, nqk)),
            full((H_RET, 1, DV_RET)),
            pl.BlockSpec((None, H_RET, DK_RET, DV_RET), lambda b, i: (b, 0, 0, 0)),
        ],
        out_specs=[
            pl.BlockSpec((tq, D_RET_V), lambda b, i: (b * nq + i, 0)),
            pl.BlockSpec((None, H_RET, DK_RET, DV_RET), lambda b, i: (b, 0, 0, 0)),
        ],
        out_shape=[jax.ShapeDtypeStruct((t, D_RET_V), out_dtype),
                   jax.ShapeDtypeStruct((nb, H_RET, DK_RET, DV_RET), F32)],
        scratch_shapes=[pltpu.VMEM((H_RET, 2 * DK_RET, DV_RET), F32)],
        compiler_params=_params(("arbitrary", "arbitrary")),
        name="retention",
    )(z, z, z, z, cos_t, sin_t, dmask, qdec, kdec, sdec, s0)


def _mem_kernel(q_ref, mk_ref, mv_ref, o_ref, *, tq):
    rows = max(tq, 16)
    q = _pad_rows(q_ref[...] * (DH ** -0.5), rows)
    for p in range(D_MEMQ // LANES):
        sl = slice(p * LANES, (p + 1) * LANES)
        q2 = q[:, sl]
        k2 = mk_ref[:, sl].astype(BF16)
        v2 = mv_ref[:, sl].astype(BF16)
        hi = _lane(q2.shape) >= DH
        outs = []
        for e in range(2):
            mine = hi if e else jnp.logical_not(hi)
            s = _dot_nt(jnp.where(mine, q2, 0.0).astype(BF16), k2)
            pe = jnp.exp(s - jnp.max(s, axis=-1, keepdims=True))
            outs.append(_dot(pe.astype(BF16), v2) / jnp.sum(pe, axis=-1, keepdims=True))
        o = jnp.where(hi, outs[1], outs[0])
        o_ref[:, sl] = o[:tq].astype(o_ref.dtype)


def mem_attend(z, col_block, mk, mv, *, seq_len, out_dtype):
    t = z.shape[0]
    nb = t // seq_len
    tq = _pick(seq_len, (512, 256, 128))
    nq = seq_len // tq
    return pl.pallas_call(
        functools.partial(_mem_kernel, tq=tq),
        grid=(nb, nq),
        in_specs=[
            pl.BlockSpec((tq, D_MEMQ), lambda b, i: (b * nq + i, col_block)),
            pl.BlockSpec((None, N_MEM, D_MEMQ), lambda b, i: (b, 0, 0)),
            pl.BlockSpec((None, N_MEM, D_MEMQ), lambda b, i: (b, 0, 0)),
        ],
        out_specs=pl.BlockSpec((tq, D_MEMQ), lambda b, i: (b * nq + i, 0)),
        out_shape=jax.ShapeDtypeStruct((t, D_MEMQ), out_dtype),
        compiler_params=_params(("arbitrary", "arbitrary")),
        name="mem_attend",
    )(z, mk, mv)


def _suffix_scan(x):
    lane = _lane(x.shape)
    t = x
    d = 1
    while d < LANES:
        t = t + jnp.where(lane + d < LANES, pltpu.roll(t, LANES - d, 1), 0.0)
        d *= 2
    return t


def _suffix_scan_rows(x):
    n = x.shape[0]
    row = lax.broadcasted_iota(jnp.int32, x.shape, 0)
    t = x
    d = 1
    while d < n:
        t = t + jnp.where(row + d < n, pltpu.roll(t, n - d, 0), 0.0)
        d *= 2
    return t


def _cumsum_kernel(x_ref, fo_ref):
    carry = jnp.zeros((HEAD_PAD, LANES), F32)
    for p in range(x_ref.shape[1] // LANES - 1, -1, -1):
        sl = slice(p * LANES, (p + 1) * LANES)
        x = x_ref[:, sl]
        inc = _suffix_scan(x)
        fo_ref[:, sl] = -(carry + (inc - x))
        carry = carry + inc[:, 0:1]


def forget_suffix(logf_t):
    nb, _, seq_len = logf_t.shape
    spec = pl.BlockSpec((None, HEAD_PAD, seq_len), lambda b: (b, 0, 0))
    return pl.pallas_call(
        _cumsum_kernel,
        grid=(nb,),
        in_specs=[spec],
        out_specs=spec,
        out_shape=jax.ShapeDtypeStruct((nb, HEAD_PAD, seq_len), F32),
        compiler_params=_params(("arbitrary",)),
        name="forget_suffix",
    )(logf_t)


def _split3(x):
    hi = x.astype(BF16).astype(F32)
    r = x - hi
    mid = r.astype(BF16).astype(F32)
    return hi, mid, r - mid


def _faug_kernel(f_ref, o_ref):
    f = f_ref[...]
    lane = _lane((f.shape[0], LANES))
    for p in range(H_FOX // 2):
        out = jnp.where((lane >= 2 * FSPLIT) & (lane < 3 * FSPLIT), 1.0, 0.0)
        for e in range(2):
            parts = _split3(f[:, 2 * p + e:2 * p + e + 1])
            for j in range(FSPLIT):
                out = jnp.where(lane == FSPLIT * e + j, parts[j], out)
        o_ref[p] = out.astype(BF16)


def forget_key_columns(f_col):
    t = f_col.shape[0]
    tm = _pick(t, (512, 256))
    npair = H_FOX // 2
    return pl.pallas_call(
        _faug_kernel,
        grid=(t // tm,),
        in_specs=[pl.BlockSpec((tm, HEAD_PAD), lambda i: (i, 0))],
        out_specs=pl.BlockSpec((npair, tm, LANES), lambda i: (0, i, 0)),
        out_shape=jax.ShapeDtypeStruct((npair, t, LANES), BF16),
        compiler_params=_params(("arbitrary",)),
        name="forget_key_columns",
    )(f_col)


SUBLANES = 8
FSPLIT = 3


def _fox_p_kernel(q_ref, kb_ref, vt_ref, fqr_ref, fa_ref, o_ref,
                  qt_ref, m_ref, l_ref, acc_ref, s4_ref, p2_ref, al_ref, *, tq, tk):
    qi = pl.program_id(1)
    ki = pl.program_id(2)
    npair = H_FOX // 2
    row_hi = lax.broadcasted_iota(jnp.int32, (LANES, tq), 0) >= DH
    nacc = 4

    @pl.when(ki == 0)
    def _():
        q = q_ref[...] * (DH ** -0.5)
        row = lax.broadcasted_iota(jnp.int32, (LANES, tq), 0)
        for p in range(npair):
            qt = q[:, p * LANES:(p + 1) * LANES].T
            for e in range(2):
                h = 2 * p + e
                qt_ref[h, 0:LANES, :] = jnp.where(row_hi == bool(e), qt, 0.0).astype(BF16)
                fq = _split3(fqr_ref[h:h + 1, :])
                ext = jnp.where((row >= FSPLIT * e) & (row < FSPLIT * (e + 1)), -1.0, 0.0)
                for j in range(FSPLIT):
                    ext = jnp.where(row == 2 * FSPLIT + j, fq[j], ext)
                qt_ref[h, LANES:2 * LANES, :] = ext.astype(BF16)
        m_ref[...] = jnp.full(m_ref.shape, NEG_INF, F32)
        l_ref[...] = jnp.zeros_like(l_ref)
        acc_ref[...] = jnp.zeros_like(acc_ref)

    def tree(op, xs):
        while len(xs) > 1:
            xs = [op(xs[i], xs[i + 1]) for i in range(0, len(xs) - 1, 2)] + ([xs[-1]] if len(xs) % 2 else [])
        return xs[0]

    def step(masked):
        def scores(p, slot):
            ka = jnp.concatenate([kb_ref[p], fa_ref[p]], axis=1)
            for e in range(2):
                s4_ref[slot + e] = _dot(ka, qt_ref[2 * p + e])

        scores(0, 0)

        def body(pp, carry):
            one_pair(2 * pp, 0)
            one_pair(2 * pp + 1, 2)
            return carry

        def one_pair(p, cur):
            vt = vt_ref[p]
            pvs = []
            scores(jnp.minimum(p + 1, npair - 1), 2 - cur)
            for e in range(2):
                h = 2 * p + e
                s_ref, p_ref = s4_ref.at[cur + e], p2_ref.at[e]
                m_all, l_all = m_ref[h], l_ref[h]
                m_news, l_news, alphas, nrows = [], [], [], []
                for c in range(tq // LANES):
                    cs = slice(c * LANES, (c + 1) * LANES)
                    nr = min(tk, LANES * (c + 1)) if masked else tk
                    nrows.append(nr)
                    mx = [jnp.full((SUBLANES, LANES), NEG_INF, F32)] * nacc
                    for r in range(nr // SUBLANES):
                        rs = slice(r * SUBLANES, (r + 1) * SUBLANES)
                        s = s_ref[rs, cs]
                        if masked and (r + 1) * SUBLANES > c * LANES:
                            key = lax.broadcasted_iota(jnp.int32, (SUBLANES, LANES), 0) + r * SUBLANES
                            qry = lax.broadcasted_iota(jnp.int32, (SUBLANES, LANES), 1) + c * LANES
                            s = jnp.where(key <= qry, s, NEG_INF)
                            s_ref[rs, cs] = s
                        mx[r % nacc] = jnp.maximum(mx[r % nacc], s)
                    m_prev = m_all[:, cs]
                    m_new = jnp.maximum(m_prev, jnp.max(tree(jnp.maximum, mx), axis=0, keepdims=True))
                    m_news.append(m_new)
                    alphas.append(jnp.exp(m_prev - m_new))
                for c in range(tq // LANES):
                    cs = slice(c * LANES, (c + 1) * LANES)
                    nr = nrows[c]
                    m16 = jnp.broadcast_to(m_news[c], (2 * SUBLANES, LANES))
                    sm = [jnp.zeros((2 * SUBLANES, LANES), F32)] * nacc
                    for r in range(nr // (2 * SUBLANES)):
                        rs = slice(r * 2 * SUBLANES, (r + 1) * 2 * SUBLANES)
                        pe = jnp.exp(s_ref[rs, cs] - m16)
                        sm[r % nacc] = sm[r % nacc] + pe
                        p_ref[rs, cs] = pe.astype(BF16)
                    if nr < tk:
                        p_ref[nr:tk, cs] = jnp.zeros((tk - nr, LANES), BF16)
                    l_news.append(alphas[c] * l_all[:, cs]
                                  + jnp.sum(tree(jnp.add, sm), axis=0, keepdims=True))
                m_ref[h] = jnp.concatenate(m_news, axis=1)
                l_ref[h] = jnp.concatenate(l_news, axis=1)
                al_ref[e] = jnp.concatenate(alphas, axis=1)
                pvs.append(_dot(vt, p_ref[...]))
            acc_ref[p] = (jnp.where(row_hi, al_ref[1], al_ref[0]) * acc_ref[p]
                          + jnp.where(row_hi, pvs[1], pvs[0]))

        lax.fori_loop(0, npair // 2, body, 0)

    @pl.when(ki < qi)
    def _():
        step(False)

    @pl.when(ki == qi)
    def _():
        step(True)
        for p in range(npair):
            denom = jnp.where(row_hi, l_ref[2 * p + 1], l_ref[2 * p])
            o_ref[:, p * LANES:(p + 1) * LANES] = (acc_ref[p] / denom).T.astype(o_ref.dtype)


def fox_prompt(z, kb, vbt, f_t, fa, *, seq_len):
    t = z.shape[0]
    nb = t // seq_len
    tq = tk = _pick(seq_len, (512, 256, 128))
    nq = seq_len // tq
    npair = H_FOX // 2
    return pl.pallas_call(
        functools.partial(_fox_p_kernel, tq=tq, tk=tk),
        grid=(nb, nq, nq),
        in_specs=[
            pl.BlockSpec((tq, D_FOX), lambda b, i, j: (b * nq + i, 0)),
            pl.BlockSpec((npair, tk, LANES), lambda b, i, j: (0, b * nq + jnp.minimum(i, j), 0)),
            pl.BlockSpec((None, npair, LANES, tk), lambda b, i, j: (b, 0, 0, jnp.minimum(i, j))),
            pl.BlockSpec((None, HEAD_PAD, tq), lambda b, i, j: (b, 0, i)),
            pl.BlockSpec((npair, tk, LANES), lambda b, i, j: (0, b * nq + jnp.minimum(i, j), 0)),
        ],
        out_specs=pl.BlockSpec((tq, D_FOX), lambda b, i, j: (b * nq + i, 0)),
        out_shape=jax.ShapeDtypeStruct((t, D_FOX), BF16),
        scratch_shapes=[
            pltpu.VMEM((H_FOX, 2 * LANES, tq), BF16),
            pltpu.VMEM((H_FOX, 1, tq), F32),
            pltpu.VMEM((H_FOX, 1, tq), F32),
            pltpu.VMEM((npair, LANES, tq), F32),
            pltpu.VMEM((4, tk, tq), F32),
            pltpu.VMEM((2, tk, tq), BF16),
            pltpu.VMEM((2, 1, tq), F32),
        ],
        compiler_params=_params(("arbitrary", "arbitrary", "arbitrary")),
        name="fox_prompt",
    )(z, kb, vbt, f_t, fa)


def _fox_s_kernel(pt_ref, q_ref, kn_ref, vn_ref, lfr_ref, lfn_ref, *rest, npg, nj, ls):
    lf_refs = rest[:npg]
    k_refs = rest[npg:2 * npg]
    v_refs = rest[2 * npg:3 * npg]
    o_ref, qbd_ref, fq_ref, carry_ref, m_ref, l_ref, acc_ref = rest[3 * npg:]
    j = pl.program_id(1)
    nrow = H_FOX * ls

    def update(s_raw, fk, mask, vs, v_transposed):
        fq = fq_ref[...]
        s = jnp.concatenate(
            [s_raw[h * ls:(h + 1) * ls, :] + fq[h * ls:(h + 1) * ls, :] - fk[h:h + 1, :]
             for h in range(H_FOX)], axis=0)
        if mask is not None:
            s = jnp.where(mask, s, NEG_INF)
        m_prev = m_ref[...]
        m_new = jnp.maximum(m_prev, jnp.max(s, axis=-1, keepdims=True))
        alpha = jnp.exp(m_prev - m_new)
        pe = jnp.exp(s - m_new)
        l_ref[...] = alpha * l_ref[...] + jnp.sum(pe, axis=-1, keepdims=True)
        m_ref[...] = m_new
        pb = pe.astype(BF16)
        mm = _dot_nt if v_transposed else _dot
        pv = mm(pb[:, :LANES], vs[0])
        for u in range(1, len(vs)):
            pv = pv + mm(pb[:, u * LANES:(u + 1) * LANES], vs[u])
        acc_ref[...] = alpha * acc_ref[...] + pv

    @pl.when(j == 0)
    def _():
        q = q_ref[...] * (DH ** -0.5)
        head = _lane(q.shape) >> 6
        qbd_ref[...] = jnp.concatenate(
            [jnp.where(head == h, q, 0.0) for h in range(H_FOX)], axis=0).astype(BF16)
        m_ref[...] = jnp.full(m_ref.shape, NEG_INF, F32)
        l_ref[...] = jnp.zeros_like(l_ref)
        acc_ref[...] = jnp.zeros_like(acc_ref)
        lfr = lfr_ref[...]
        fq_rows = -(_suffix_scan_rows(lfr) - lfr)
        fq_ref[...] = jnp.concatenate([fq_rows[:, h:h + 1] for h in range(H_FOX)], axis=0)
        lfn = lfn_ref[...]
        inc = _suffix_scan(lfn)
        carry_ref[...] = jnp.broadcast_to(inc[:, 0:1], carry_ref.shape)
        kn = _pad_rows(kn_ref[...], LANES).astype(BF16)
        vn = _pad_rows(vn_ref[...], LANES).astype(BF16)
        col = _lane((nrow, LANES))
        qrow = lax.broadcasted_iota(jnp.int32, (nrow, LANES), 0) & (ls - 1)
        update(_dot_nt(qbd_ref[...], kn), -(inc - lfn), col <= qrow, [vn], False)

    carry = carry_ref[...]
    fks = []
    for u in range(npg):
        x = lf_refs[u][...]
        inc = _suffix_scan(x)
        fks.append(-(carry + (inc - x)))
        carry = carry + inc[:, 0:1]
    carry_ref[...] = carry
    qbd = qbd_ref[...]
    s_raw = jnp.concatenate([_dot(qbd, k_refs[u][...].astype(BF16)) for u in range(npg)], axis=1)
    update(s_raw, jnp.concatenate(fks, axis=1), None,
           [v_refs[u][...].astype(BF16) for u in range(npg)], True)

    @pl.when(j == nj - 1)
    def _():
        acc = acc_ref[...] / l_ref[...]
        head = _lane((ls, D_FOX)) >> 6
        out = jnp.zeros((ls, D_FOX), F32)
        for h in range(H_FOX):
            out = out + jnp.where(head == h, acc[h * ls:(h + 1) * ls, :], 0.0)
        o_ref[...] = out


def fox_sample(z, k_new, v_new, lf_rows, lf_new_t, cache_lf_t, cache_kt, cache_vt, page_table, *, seq_len):
    t = z.shape[0]
    ls = seq_len
    nb = t // ls
    n_pages = page_table.shape[1]
    npg = _pick(n_pages, (16, 8, 4, 2))
    nj = n_pages // npg
    page = cache_kt.shape[2]
    nrow = H_FOX * ls

    def page_spec(u, rows):
        return pl.BlockSpec((None, rows, page),
                            lambda b, j, pt: (pt[b, n_pages - 1 - (j * npg + u)], 0, 0))

    row_spec = pl.BlockSpec((ls, D_FOX), lambda b, j, pt: (b, 0))
    return pl.pallas_call(
        functools.partial(_fox_s_kernel, npg=npg, nj=nj, ls=ls),
        grid_spec=pltpu.PrefetchScalarGridSpec(
            num_scalar_prefetch=1,
            grid=(nb, nj),
            in_specs=[
                row_spec, row_spec, row_spec,
                pl.BlockSpec((ls, LANES), lambda b, j, pt: (b, 0)),
                pl.BlockSpec((None, HEAD_PAD, LANES), lambda b, j, pt: (b, 0, 0)),
            ] + [page_spec(u, HEAD_PAD) for u in range(npg)]
              + [page_spec(u, D_FOX) for u in range(npg)] * 2,
            out_specs=row_spec,
            scratch_shapes=[
                pltpu.VMEM((nrow, D_FOX), BF16),
                pltpu.VMEM((nrow, 1), F32),
                pltpu.VMEM((HEAD_PAD, LANES), F32),
                pltpu.VMEM((nrow, 1), F32),
                pltpu.VMEM((nrow, 1), F32),
                pltpu.VMEM((nrow, D_FOX), F32),
            ]),
        out_shape=jax.ShapeDtypeStruct((t, D_FOX), F32),
        compiler_params=_params(("arbitrary", "arbitrary")),
        name="fox_sample",
    )(page_table, z, k_new, v_new, lf_rows, lf_new_t,
      *([cache_lf_t] * npg), *([cache_kt] * npg), *([cache_vt] * npg))


def _heads_to_sublanes(logf, nb, seq_len):
    x = logf[:, :H_FOX].reshape(nb, seq_len, H_FOX).transpose(0, 2, 1)
    return jnp.pad(x, ((0, 0), (0, HEAD_PAD - H_FOX), (0, 0)))


def _trunk(x, pos, mem_k, mem_v, ret_state, conv_state, past, p):
    nb, seq_len, d = x.shape
    t = nb * seq_len
    sample = past is not None
    cat_dtype = F32 if sample else BF16
    h = x.reshape(t, d)
    new_ret, new_conv = [], []
    k_sh = v_sh = logf_sh = None
    for layer in range(DEPTH):
        if layer < N_A:
            z = norm_matmul(h, p['g_mix_pre'][layer], p['w_in_a'], layer)
            s0 = (jnp.zeros((nb, H_RET, DK_RET, DV_RET), F32) if ret_state is None else ret_state[layer])
            r, s_new = retention(z, s0, pos, seq_len=seq_len, out_dtype=cat_dtype)
            new_ret.append(s_new)
            m = mem_attend(z, (2 * D_RET_QK + 2 * D_RET_V) // D_MEMQ, mem_k[layer], mem_v[layer],
                           seq_len=seq_len, out_dtype=cat_dtype)
            h = out_proj_residual(r, m, p['w_out_a'], layer, p['g_mix_post'][layer], h)
        else:
            if layer == N_A:
                k_sh, v_sh, lf, *attn = shared_kv(h, p['g_kv'], p['wk'], p['wv'], p['wf'], p['bf'],
                                                  seq_len=seq_len, attn_copies=not sample)
                logf_sh = lf[:, :H_FOX]
                lf_t = _heads_to_sublanes(lf, nb, seq_len)
                if not sample:
                    kb, vbt = attn
                    f_t = forget_suffix(lf_t)
                    fa = forget_key_columns(f_t.transpose(0, 2, 1).reshape(t, HEAD_PAD))
                else:
                    cache_kt, cache_vt, cache_lf_t, page_table = past
                    lf_new_t = jnp.pad(lf_t, ((0, 0), (0, 0), (0, LANES - seq_len)))
            j = layer - N_A
            z = norm_matmul(h, p['g_mix_pre'][layer], p['w_in_b'], j)
            if not sample:
                o = fox_prompt(z, kb, vbt, f_t, fa, seq_len=seq_len)
            else:
                o = fox_sample(z, k_sh, v_sh, lf, lf_new_t, cache_lf_t, cache_kt, cache_vt, page_table,
                               seq_len=seq_len)
            m = mem_attend(z, D_FOX // D_MEMQ, mem_k[layer], mem_v[layer],
                           seq_len=seq_len, out_dtype=cat_dtype)
            h = out_proj_residual(o, m, p['w_out_b'], j, p['g_mix_post'][layer], h)
        h, cs = conv_ffn(h, layer, p, seq_len=seq_len,
                         conv_state=None if conv_state is None else conv_state[layer])
        new_conv.append(cs)
    y = h.reshape(nb, seq_len, d)
    k_o = k_sh.reshape(nb, seq_len, H_FOX, DH)
    v_o = v_sh.reshape(nb, seq_len, H_FOX, DH)
    logf_o = logf_sh.reshape(nb, seq_len, H_FOX)
    return y, k_o, v_o, logf_o, jnp.stack(new_ret), jnp.stack(new_conv)


def kernel(x_prompt, x_sample, mem_prompt, cache_k, cache_v, cache_logf, page_table, cache_mem_k, cache_mem_v, state_ret, state_conv, g_mix_pre, g_mix_post, g_ffn_pre, g_ffn_post, w_in_a, w_out_a, w_in_b, w_out_b, w_mem_kv, g_kv, w_kv_shared, b_f, w_ffn_up, conv_w, conv_b, w_ffn_down):
    wf = jnp.pad(w_kv_shared[:, 2 * D_FOX:], ((0, 0), (0, LANES - H_FOX))).astype(BF16)
    bf = jnp.pad(b_f, (0, LANES - H_FOX)).reshape(1, LANES)
    p = {'g_mix_pre': g_mix_pre, 'g_mix_post': g_mix_post, 'g_ffn_pre': g_ffn_pre, 'g_ffn_post': g_ffn_post,
         'w_in_a': w_in_a.astype(BF16), 'w_out_a': w_out_a.astype(BF16),
         'w_in_b': w_in_b.astype(BF16), 'w_out_b': w_out_b.astype(BF16),
         'g_kv': g_kv, 'wk': w_kv_shared[:, :D_FOX].astype(BF16),
         'wv': w_kv_shared[:, D_FOX:2 * D_FOX].astype(BF16), 'wf': wf, 'bf': bf,
         'w_ffn_up': w_ffn_up.astype(BF16), 'conv_w': conv_w, 'conv_b': conv_b,
         'w_ffn_down': w_ffn_down.astype(BF16)}

    nb, seq_len, d = x_prompt.shape
    n_mem = mem_prompt.shape[1]
    mk, mv = mem_kv(mem_prompt.reshape(nb * n_mem, d), w_mem_kv.astype(BF16))
    mk = mk.reshape(DEPTH, nb, n_mem, D_MEMQ)
    mv = mv.reshape(DEPTH, nb, n_mem, D_MEMQ)
    pos_p = jnp.arange(seq_len, dtype=jnp.int32)
    y_p, k_p, v_p, logf_p, ret_p, conv_p = _trunk(x_prompt, pos_p, mk, mv, None, None, None, p)
    mem_k_p = mk.reshape(DEPTH, nb, n_mem, H_MEM, DH)
    mem_v_p = mv.reshape(DEPTH, nb, n_mem, H_MEM, DH)

    db, ls, _ = x_sample.shape
    n_phys, page = cache_k.shape[:2]
    past_len = page_table.shape[1] * page
    pos_s = past_len + jnp.arange(ls, dtype=jnp.int32)
    cache_lf_t = jnp.pad(cache_logf.transpose(0, 2, 1), ((0, 0), (0, HEAD_PAD - H_FOX), (0, 0)))
    cache_kt = cache_k.transpose(0, 2, 3, 1).reshape(n_phys, D_FOX, page)
    cache_vt = cache_v.transpose(0, 2, 3, 1).reshape(n_phys, D_FOX, page)
    past = (cache_kt, cache_vt, cache_lf_t, page_table)
    y_s, k_s, v_s, logf_s, ret_s, conv_s = _trunk(
        x_sample, pos_s, cache_mem_k.reshape(DEPTH, db, N_MEM, D_MEMQ),
        cache_mem_v.reshape(DEPTH, db, N_MEM, D_MEMQ), state_ret, state_conv, past, p)

    return (y_p, y_s, k_p, v_p, logf_p, k_s, v_s, logf_s,
            mem_k_p, mem_v_p, ret_p, ret_s, conv_p, conv_s)
```

```python
import functools

import jax
import jax.numpy as jnp
from jax import lax
from jax.experimental import pallas as pl
from jax.experimental.pallas import tpu as pltpu

F32 = jnp.float32
BF16 = jnp.bfloat16

D_MODEL = 1024
DEPTH = 4
N_A = 2
H_RET = 6
DK_RET = 64
DV_RET = 128
H_FOX = 12
DH = 64
H_MEM = 4
N_MEM = 256
D_RET_QK = H_RET * DK_RET
D_RET_V = H_RET * DV_RET
D_FOX = H_FOX * DH
D_MEMQ = H_MEM * DH
D_FF = 2816
RET_CHUNK = 128
ROPE_BASE = 10000.0
EPS = 1e-6

LANES = 128
HEAD_PAD = 16
VMEM_LIMIT = 48 * 1024 * 1024

NEG_INF = float("-inf")


def _params(sem):
    return pltpu.CompilerParams(dimension_semantics=sem, vmem_limit_bytes=VMEM_LIMIT)


def _rms(x, g):
    return x * lax.rsqrt(jnp.mean(x * x, axis=-1, keepdims=True) + EPS) * g


def _dot(a, b):
    return jnp.dot(a, b, preferred_element_type=F32)


def _dot_nt(a, b):
    return lax.dot_general(a, b, (((1,), (1,)), ((), ())), preferred_element_type=F32)


def _lane(shape):
    return lax.broadcasted_iota(jnp.int32, shape, len(shape) - 1)


def _pad_rows(x, n):
    if x.shape[0] == n:
        return x
    return jnp.concatenate([x, jnp.zeros((n - x.shape[0],) + x.shape[1:], x.dtype)], axis=0)


def _pick(n, cands):
    for c in cands:
        if n % c == 0:
            return c
    return n


NM_WEIGHT_BYTES = 6 * 1024 * 1024


def _nm_kernel(x_ref, g_ref, w_ref, o_ref, xn_ref, *, norm):
    @pl.when(pl.program_id(1) == 0)
    def _():
        x = x_ref[...]
        if norm:
            x = _rms(x, g_ref[...])
        xn_ref[...] = x.astype(BF16)

    o_ref[...] = _dot(xn_ref[...], w_ref[...])


def norm_matmul(x, g, w, layer, *, norm=True):
    t, d = x.shape
    n = w.shape[-1]
    tm = _pick(t, (512, 256))
    tn = n if d * n * 2 <= NM_WEIGHT_BYTES else _pick(n, (512, 256, 128))
    return pl.pallas_call(
        functools.partial(_nm_kernel, norm=norm),
        grid=(t // tm, n // tn),
        in_specs=[
            pl.BlockSpec((tm, d), lambda i, j: (i, 0)),
            pl.BlockSpec((1, d), lambda i, j: (0, 0)),
            pl.BlockSpec((None, d, tn), lambda i, j: (layer, 0, j)),
        ],
        out_specs=pl.BlockSpec((tm, tn), lambda i, j: (i, j)),
        out_shape=jax.ShapeDtypeStruct((t, n), F32),
        scratch_shapes=[pltpu.VMEM((tm, d), BF16)],
        compiler_params=_params(("arbitrary", "arbitrary")),
        name="norm_matmul",
    )(x, g.reshape(1, d), w)


def _memkv_kernel(x_ref, w_ref, k_ref, v_ref):
    acc = _dot(x_ref[...].astype(BF16), w_ref[...])
    k_ref[...] = acc[:, :D_MEMQ]
    v_ref[...] = acc[:, D_MEMQ:]


def mem_kv(x, w):
    t, d = x.shape
    tm = _pick(t, (1024, 512, 256))
    out = jax.ShapeDtypeStruct((DEPTH, t, D_MEMQ), F32)
    return pl.pallas_call(
        _memkv_kernel,
        grid=(t // tm, DEPTH),
        in_specs=[
            pl.BlockSpec((tm, d), lambda i, l: (i, 0)),
            pl.BlockSpec((None, d, 2 * D_MEMQ), lambda i, l: (l, 0, 0)),
        ],
        out_specs=[pl.BlockSpec((None, tm, D_MEMQ), lambda i, l: (l, i, 0))] * 2,
        out_shape=[out, out],
        compiler_params=_params(("arbitrary", "arbitrary")),
        name="mem_kv",
    )(x, w)


def _log_sigmoid(x):
    return jnp.minimum(x, 0.0) - jnp.log1p(jnp.exp(-jnp.abs(x)))


def _kv_kernel(x_ref, g_ref, wk_ref, wv_ref, wf_ref, bf_ref, k_ref, v_ref, lf_ref, *attn_refs):
    xn = _rms(x_ref[...], g_ref[...]).astype(BF16)
    k = _dot(xn, wk_ref[...])
    v = _dot(xn, wv_ref[...])
    k_ref[...] = k
    v_ref[...] = v
    lf_ref[...] = _log_sigmoid(_dot(xn, wf_ref[...]) + bf_ref[...])
    if attn_refs:
        kb_ref, vbt_ref = attn_refs
        for p in range(D_FOX // LANES):
            kb_ref[p] = k[:, p * LANES:(p + 1) * LANES].astype(BF16)
            vbt_ref[p] = v[:, p * LANES:(p + 1) * LANES].T.astype(BF16)


def shared_kv(h, g, wk, wv, wf, bf, *, seq_len, attn_copies):
    t, d = h.shape
    tm = _pick(t, (512, 256))
    npair = D_FOX // LANES
    full = lambda shape: pl.BlockSpec(shape, lambda i: (0,) * len(shape))
    out_specs = [
        pl.BlockSpec((tm, D_FOX), lambda i: (i, 0)),
        pl.BlockSpec((tm, D_FOX), lambda i: (i, 0)),
        pl.BlockSpec((tm, LANES), lambda i: (i, 0)),
    ]
    out_shape = [
        jax.ShapeDtypeStruct((t, D_FOX), F32),
        jax.ShapeDtypeStruct((t, D_FOX), F32),
        jax.ShapeDtypeStruct((t, LANES), F32),
    ]
    if attn_copies:
        tpb = seq_len // tm
        out_specs += [
            pl.BlockSpec((npair, tm, LANES), lambda i: (0, i, 0)),
            pl.BlockSpec((None, npair, LANES, tm), lambda i: (i // tpb, 0, 0, i % tpb)),
        ]
        out_shape += [
            jax.ShapeDtypeStruct((npair, t, LANES), BF16),
            jax.ShapeDtypeStruct((t // seq_len, npair, LANES, seq_len), BF16),
        ]
    return pl.pallas_call(
        _kv_kernel,
        grid=(t // tm,),
        in_specs=[
            pl.BlockSpec((tm, d), lambda i: (i, 0)),
            full((1, d)), full((d, D_FOX)), full((d, D_FOX)), full((d, LANES)), full((1, LANES)),
        ],
        out_specs=out_specs,
        out_shape=out_shape,
        compiler_params=_params(("arbitrary",)),
        name="shared_kv",
    )(h, g.reshape(1, d), wk, wv, wf, bf)


FFN_COLS = 1408
FFN_ROWS = 16
FFN_CHUNK = 128


def _gelu_tanh(x):
    return 0.5 * x * (1.0 + jnp.tanh(0.7978845608028654 * (x + 0.044715 * (x * x * x))))


def _ffn_kernel(*refs, tm, tf, nf, tiles_per_batch, seq_rows):
    (c1_ref, c2_ref, wo_ref, gmix_ref,
     h_ref, gpre_ref, wg_ref, wv_ref, cwg_ref, cwv_ref, cbg_ref, cbv_ref, wd_ref, gpost_ref) = refs[:14]
    if seq_rows is None:
        o_ref, tg_ref, tv_ref, hm_ref, xn_ref, acc_ref, ug_ref, uv_ref, hid_ref, cg_ref, cv_ref = refs[14:]
        p1g_ref = p1v_ref = p2g_ref = p2v_ref = None
    else:
        (p1g_ref, p1v_ref, p2g_ref, p2v_ref, o_ref, tg_ref, tv_ref,
         hm_ref, xn_ref, acc_ref, ug_ref, uv_ref, hid_ref) = refs[14:]
        cg_ref = cv_ref = None
    i = pl.program_id(0)
    f = pl.program_id(1)
    rg = FFN_ROWS

    ch = FFN_CHUNK

    @pl.when(f == 0)
    def _():
        n1 = c1_ref.shape[1]
        mix = (_dot(c1_ref[...].astype(BF16), wo_ref[:n1, :])
               + _dot(c2_ref[...].astype(BF16), wo_ref[n1:, :]))
        hm = h_ref[...] + _rms(mix, gmix_ref[...])
        hm_ref[...] = hm
        xn_ref[...] = _rms(hm, gpre_ref[...]).astype(BF16)
        acc_ref[...] = jnp.zeros_like(acc_ref)

    for u_ref, c_ref in ((ug_ref, cg_ref), (uv_ref, cv_ref)):
        if seq_rows is None:
            @pl.when((i % tiles_per_batch) == 0)
            def _():
                c_ref[f] = jnp.zeros((8, tf), F32)

            u_ref[0:8, :] = c_ref[f]
        else:
            u_ref[0:8, :] = jnp.zeros((8, tf), F32)

    if seq_rows is not None:
        r = lax.broadcasted_iota(jnp.int32, (rg, LANES), 0) & (seq_rows - 1)
        keep1 = r >= 1
        keep2 = r >= 2

    def up(c):
        xc = xn_ref[c * ch:(c + 1) * ch, :]
        ug_ref[8 + c * ch:8 + (c + 1) * ch, :] = _dot(xc, wg_ref[...])
        uv_ref[8 + c * ch:8 + (c + 1) * ch, :] = _dot(xc, wv_ref[...])

    def conv(u_ref, cw_ref, cb_ref, p1_ref, p2_ref, r0, sl):
        a = u_ref[r0:r0 + rg + 8, sl]
        u = a[8:]
        u1 = pltpu.roll(a, 1, 0)[8:]
        u2 = pltpu.roll(a, 2, 0)[8:]
        if seq_rows is not None:
            u1 = jnp.where(keep1, u1, p1_ref[r0:r0 + rg, sl])
            u2 = jnp.where(keep2, u2, p2_ref[r0:r0 + rg, sl])
        return cb_ref[:, sl] + cw_ref[0:1, sl] * u2 + cw_ref[1:2, sl] * u1 + cw_ref[2:3, sl] * u

    def gate_mul(c):
        for lt in range(tf // LANES):
            sl = slice(lt * LANES, (lt + 1) * LANES)
            for r0 in range(c * ch, (c + 1) * ch, rg):
                gate = conv(ug_ref, cwg_ref, cbg_ref, p1g_ref, p2g_ref, r0, sl)
                val = conv(uv_ref, cwv_ref, cbv_ref, p1v_ref, p2v_ref, r0, sl)
                hid_ref[r0:r0 + rg, sl] = (_gelu_tanh(gate) * val).astype(BF16)

    def down(c):
        rows = slice(c * ch, (c + 1) * ch)
        acc_ref[rows, :] += _dot(hid_ref[rows, :], wd_ref[...])

    nch = tm // ch
    ahead = 2
    for c in range(min(ahead, nch)):
        up(c)
    for c in range(nch):
        if c + ahead < nch:
            up(c + ahead)
        gate_mul(c)
        down(c)

    if seq_rows is None:
        cg_ref[f] = ug_ref[tm:tm + 8, :]
        cv_ref[f] = uv_ref[tm:tm + 8, :]
        tg_ref[0] = ug_ref[tm:tm + 8, :]
        tv_ref[0] = uv_ref[tm:tm + 8, :]
    else:
        tg_ref[...] = ug_ref[8:tm + 8, :]
        tv_ref[...] = uv_ref[8:tm + 8, :]

    @pl.when(f == nf - 1)
    def _():
        o_ref[...] = hm_ref[...] + _rms(acc_ref[...], gpost_ref[...])


def mix_out_conv_ffn(c1, c2, w_out, w_layer, g_mix, h, layer, p, *, seq_len, conv_state=None):
    t, d = h.shape
    n1, n2 = c1.shape[1], c2.shape[1]
    nb = t // seq_len
    tf = FFN_COLS
    nf = D_FF // tf
    w_up, cw, cb, w_down = p['w_ffn_up'], p['conv_w'], p['conv_b'], p['w_ffn_down']
    cb3 = cb.reshape(DEPTH, 1, 2 * D_FF)
    if conv_state is None:
        tm = _pick(seq_len, (512, 256))
        tiles_per_batch = seq_len // tm
        seq_rows = None
        tail_shape = jax.ShapeDtypeStruct((t // tm, 8, D_FF), F32)
        tail_spec = pl.BlockSpec((1, 8, tf), lambda i, f: (i, 0, f))
        extra_in, extra_specs = [], []
        extra_scratch = [pltpu.VMEM((nf, 8, tf), F32), pltpu.VMEM((nf, 8, tf), F32)]
    else:
        tm = t
        tiles_per_batch = 1
        seq_rows = seq_len
        tail_shape = jax.ShapeDtypeStruct((t, D_FF), F32)
        tail_spec = pl.BlockSpec((tm, tf), lambda i, f: (i, f))
        st = conv_state
        z = jnp.zeros((nb, seq_len, 2 * D_FF), F32)
        prev1 = z.at[:, 0].set(st[:, 1]).reshape(t, 2 * D_FF)
        prev2 = z.at[:, 0].set(st[:, 0]).at[:, 1].set(st[:, 1]).reshape(t, 2 * D_FF)
        extra_in = [prev1, prev1, prev2, prev2]
        gspec = pl.BlockSpec((tm, tf), lambda i, f: (i, f))
        vspec = pl.BlockSpec((tm, tf), lambda i, f: (i, nf + f))
        extra_specs = [gspec, vspec, gspec, vspec]
        extra_scratch = []
    kern = functools.partial(_ffn_kernel, tm=tm, tf=tf, nf=nf, tiles_per_batch=tiles_per_batch,
                             seq_rows=seq_rows)
    h_new, tail_g, tail_v = pl.pallas_call(
        kern,
        grid=(t // tm, nf),
        in_specs=[
            pl.BlockSpec((tm, n1), lambda i, f: (i, 0)),
            pl.BlockSpec((tm, n2), lambda i, f: (i, 0)),
            pl.BlockSpec((None, n1 + n2, d), lambda i, f: (w_layer, 0, 0)),
            pl.BlockSpec((1, d), lambda i, f: (0, 0)),
            pl.BlockSpec((tm, d), lambda i, f: (i, 0)),
            pl.BlockSpec((1, d), lambda i, f: (0, 0)),
            pl.BlockSpec((None, d, tf), lambda i, f: (layer, 0, f)),
            pl.BlockSpec((None, d, tf), lambda i, f: (layer, 0, nf + f)),
            pl.BlockSpec((None, 3, tf), lambda i, f: (layer, 0, f)),
            pl.BlockSpec((None, 3, tf), lambda i, f: (layer, 0, nf + f)),
            pl.BlockSpec((None, 1, tf), lambda i, f: (layer, 0, f)),
            pl.BlockSpec((None, 1, tf), lambda i, f: (layer, 0, nf + f)),
            pl.BlockSpec((None, tf, d), lambda i, f: (layer, f, 0)),
            pl.BlockSpec((1, d), lambda i, f: (0, 0)),
        ] + extra_specs,
        out_specs=[pl.BlockSpec((tm, d), lambda i, f: (i, 0)), tail_spec, tail_spec],
        out_shape=[jax.ShapeDtypeStruct((t, d), F32), tail_shape, tail_shape],
        scratch_shapes=[pltpu.VMEM((tm, d), F32), pltpu.VMEM((tm, d), BF16), pltpu.VMEM((tm, d), F32),
                        pltpu.VMEM((tm + 8, tf), F32), pltpu.VMEM((tm + 8, tf), F32),
                        pltpu.VMEM((tm, tf), BF16)] + extra_scratch,
        compiler_params=_params(("arbitrary", "arbitrary")),
        name="conv_ffn",
    )(c1, c2, w_out, g_mix.reshape(1, d), h, p['g_ffn_pre'][layer].reshape(1, d), w_up, w_up, cw, cw,
      cb3, cb3, w_down, p['g_ffn_post'][layer].reshape(1, d), *extra_in)
    if conv_state is None:
        last = slice(tiles_per_batch - 1, None, tiles_per_batch)
        new_state = jnp.concatenate([tail_g[last, 6:8], tail_v[last, 6:8]], axis=-1)
    else:
        new_state = jnp.concatenate([tail_g.reshape(nb, seq_len, D_FF)[:, seq_len - 2:],
                                     tail_v.reshape(nb, seq_len, D_FF)[:, seq_len - 2:]], axis=-1)
    return h_new, new_state


def _rotary(x, cos, sins):
    parts = []
    for p in range(x.shape[1] // LANES):
        xb = x[:, p * LANES:(p + 1) * LANES]
        first = (_lane(xb.shape) & (DK_RET // 2)) == 0
        sw = jnp.where(first, pltpu.roll(xb, LANES - DK_RET // 2, 1), pltpu.roll(xb, DK_RET // 2, 1))
        parts.append(xb * cos[:, p * LANES:(p + 1) * LANES] + sw * sins[:, p * LANES:(p + 1) * LANES])
    return jnp.concatenate(parts, axis=1)


def _ret_kernel(q_ref, k_ref, v_ref, g_ref, cos_ref, sin_ref, dm_ref, qd_ref, kd_ref, sd_ref, s0_ref,
                r_ref, so_ref, s_ref, *, tq, nq):
    c = RET_CHUNK
    qi = pl.program_id(1)
    zero_half = jnp.zeros((DK_RET, DV_RET), F32)

    @pl.when(qi == 0)
    def _():
        for h in range(H_RET):
            halves = [zero_half, zero_half]
            halves[h % 2] = s0_ref[h]
            s_ref[h] = jnp.concatenate(halves, axis=0)

    cos = cos_ref[...]
    sins = sin_ref[...]
    q = _rotary(q_ref[...], cos, sins)
    k = _rotary(k_ref[...], cos, sins) * (DK_RET ** -0.5)
    rows = max(tq, c)
    for cs in range(0, rows, c):
        n = min(c, tq)
        qc = _pad_rows(q[cs:cs + n], c)
        kc = _pad_rows(k[cs:cs + n], c)
        vc = _pad_rows(v_ref[cs:cs + n, :], c)
        qd = qc * qd_ref[...]
        kd = kc * kd_ref[...]
        for p in range(H_RET // 2):
            sl = slice(p * LANES, (p + 1) * LANES)
            q2, k2, qd2, kd2 = qc[:, sl], kc[:, sl], qd[:, sl], kd[:, sl]
            k2b = k2.astype(BF16)
            hi = _lane(q2.shape) >= DK_RET
            for e in range(2):
                h = 2 * p + e
                mine = hi if e else jnp.logical_not(hi)
                qm = jnp.where(mine, q2, 0.0).astype(BF16)
                qdm = jnp.where(mine, qd2, 0.0).astype(BF16)
                kdm = jnp.where(mine, kd2, 0.0)
                vh = vc[:, h * DV_RET:(h + 1) * DV_RET].astype(BF16)
                inner = (_dot_nt(qm, k2b) * dm_ref[h]).astype(BF16)
                s_old = s_ref[h]
                o = _dot(inner, vh) + _dot(qdm, s_old.astype(BF16))
                s_ref[h] = sd_ref[h] * s_old + _dot(kdm.T.astype(BF16), vh)
                cen = o - jnp.mean(o, axis=-1, keepdims=True)
                y = cen * lax.rsqrt(jnp.mean(cen * cen, axis=-1, keepdims=True) + EPS)
                gh = g_ref[cs:cs + n, h * DV_RET:(h + 1) * DV_RET]
                r = (gh / (1.0 + jnp.exp(-gh))) * y[:n]
                r_ref[cs:cs + n, h * DV_RET:(h + 1) * DV_RET] = r.astype(r_ref.dtype)

    @pl.when(qi == nq - 1)
    def _():
        for h in range(H_RET):
            e = h % 2
            so_ref[h] = s_ref[h][e * DK_RET:(e + 1) * DK_RET, :]


def _ret_tables(chunk):
    c = RET_CHUNK
    log_gamma = jnp.log1p(-(2.0 ** (-5.0 - jnp.arange(H_RET, dtype=F32))))
    idx = jnp.arange(c, dtype=F32)
    rel = idx[:, None] - idx[None, :]
    dmask = jnp.where(rel >= 0, jnp.exp(log_gamma[:, None, None] * jnp.maximum(rel, 0.0)), 0.0)
    qdec = jnp.exp(log_gamma[None, :] * (idx[:, None] + 1.0))
    kdec = jnp.exp(log_gamma[None, :] * (chunk - 1.0 - idx[:, None]))
    sdec = jnp.exp(log_gamma * chunk)
    qdec = jnp.repeat(qdec, DK_RET, axis=1)
    kdec = jnp.repeat(kdec, DK_RET, axis=1)
    sdec = jnp.broadcast_to(sdec[:, None, None], (H_RET, 1, DV_RET))
    return dmask, qdec, kdec, sdec


def _rope_tables(pos):
    half = DK_RET // 2
    inv = ROPE_BASE ** (-jnp.arange(half, dtype=F32) / half)
    ang = pos.astype(F32)[:, None] * inv[None, :]
    cos, sin = jnp.cos(ang), jnp.sin(ang)
    cos_t = jnp.tile(jnp.concatenate([cos, cos], axis=1), (1, H_RET))
    sin_t = jnp.tile(jnp.concatenate([-sin, sin], axis=1), (1, H_RET))
    return cos_t, sin_t


def retention(z, s0, pos, *, seq_len, out_dtype):
    t = z.shape[0]
    nb = t // seq_len
    chunk = RET_CHUNK if seq_len % RET_CHUNK == 0 else seq_len
    tq = _pick(seq_len, (512, 256, 128))
    nq = seq_len // tq
    dmask, qdec, kdec, sdec = _ret_tables(chunk)
    cos_t, sin_t = _rope_tables(pos)
    nqk = D_RET_QK
    full = lambda shape: pl.BlockSpec(shape, lambda b, i: (0,) * len(shape))
    return pl.pallas_call(
        functools.partial(_ret_kernel, tq=tq, nq=nq),
        grid=(nb, nq),
        in_specs=[
            pl.BlockSpec((tq, nqk), lambda b, i: (b * nq + i, 0)),
            pl.BlockSpec((tq, nqk), lambda b, i: (b * nq + i, 1)),
            pl.BlockSpec((tq, D_RET_V), lambda b, i: (b * nq + i, 1)),
            pl.BlockSpec((tq, D_RET_V), lambda b, i: (b * nq + i, 2)),
            pl.BlockSpec((tq, nqk), lambda b, i: (i, 0)),
            pl.BlockSpec((tq, nqk), lambda b, i: (i, 0)),
            full((H_RET, RET_CHUNK, RET_CHUNK)),
            full((RET_CHUNK, nqk)),
            full((RET_CHUNK, nqk)),
            full((H_RET, 1, DV_RET)),
            pl.BlockSpec((None, H_RET, DK_RET, DV_RET), lambda b, i: (b, 0, 0, 0)),
        ],
        out_specs=[
            pl.BlockSpec((tq, D_RET_V), lambda b, i: (b * nq + i, 0)),
            pl.BlockSpec((None, H_RET, DK_RET, DV_RET), lambda b, i: (b, 0, 0, 0)),
        ],
        out_shape=[jax.ShapeDtypeStruct((t, D_RET_V), out_dtype),
                   jax.ShapeDtypeStruct((nb, H_RET, DK_RET, DV_RET), F32)],
        scratch_shapes=[pltpu.VMEM((H_RET, 2 * DK_RET, DV_RET), F32)],
        compiler_params=_params(("arbitrary", "arbitrary")),
        name="retention",
    )(z, z, z, z, cos_t, sin_t, dmask, qdec, kdec, sdec, s0)


def _mem_kernel(q_ref, mk_ref, mv_ref, o_ref, *, tq):
    rows = max(tq, 16)
    q = _pad_rows(q_ref[...] * (DH ** -0.5), rows)
    for p in range(D_MEMQ // LANES):
        sl = slice(p * LANES, (p + 1) * LANES)
        q2 = q[:, sl]
        k2 = mk_ref[:, sl].astype(BF16)
        v2 = mv_ref[:, sl].astype(BF16)
        hi = _lane(q2.shape) >= DH
        outs = []
        for e in range(2):
            mine = hi if e else jnp.logical_not(hi)
            s = _dot_nt(jnp.where(mine, q2, 0.0).astype(BF16), k2)
            pe = jnp.exp(s - jnp.max(s, axis=-1, keepdims=True))
            outs.append(_dot(pe.astype(BF16), v2) / jnp.sum(pe, axis=-1, keepdims=True))
        o = jnp.where(hi, outs[1], outs[0])
        o_ref[:, sl] = o[:tq].astype(o_ref.dtype)


def mem_attend(z, col_block, mk, mv, *, seq_len, out_dtype):
    t = z.shape[0]
    nb = t // seq_len
    tq = _pick(seq_len, (512, 256, 128))
    nq = seq_len // tq
    return pl.pallas_call(
        functools.partial(_mem_kernel, tq=tq),
        grid=(nb, nq),
        in_specs=[
            pl.BlockSpec((tq, D_MEMQ), lambda b, i: (b * nq + i, col_block)),
            pl.BlockSpec((None, N_MEM, D_MEMQ), lambda b, i: (b, 0, 0)),
            pl.BlockSpec((None, N_MEM, D_MEMQ), lambda b, i: (b, 0, 0)),
        ],
        out_specs=pl.BlockSpec((tq, D_MEMQ), lambda b, i: (b * nq + i, 0)),
        out_shape=jax.ShapeDtypeStruct((t, D_MEMQ), out_dtype),
        compiler_params=_params(("arbitrary", "arbitrary")),
        name="mem_attend",
    )(z, mk, mv)


def _suffix_scan(x):
    lane = _lane(x.shape)
    t = x
    d = 1
    while d < LANES:
        t = t + jnp.where(lane + d < LANES, pltpu.roll(t, LANES - d, 1), 0.0)
        d *= 2
    return t


def _suffix_scan_rows(x):
    n = x.shape[0]
    row = lax.broadcasted_iota(jnp.int32, x.shape, 0)
    t = x
    d = 1
    while d < n:
        t = t + jnp.where(row + d < n, pltpu.roll(t, n - d, 0), 0.0)
        d *= 2
    return t


def _cumsum_kernel(x_ref, fo_ref):
    carry = jnp.zeros((HEAD_PAD, LANES), F32)
    for p in range(x_ref.shape[1] // LANES - 1, -1, -1):
        sl = slice(p * LANES, (p + 1) * LANES)
        x = x_ref[:, sl]
        inc = _suffix_scan(x)
        fo_ref[:, sl] = -(carry + (inc - x))
        carry = carry + inc[:, 0:1]


def forget_suffix(logf_t):
    nb, _, seq_len = logf_t.shape
    spec = pl.BlockSpec((None, HEAD_PAD, seq_len), lambda b: (b, 0, 0))
    return pl.pallas_call(
        _cumsum_kernel,
        grid=(nb,),
        in_specs=[spec],
        out_specs=spec,
        out_shape=jax.ShapeDtypeStruct((nb, HEAD_PAD, seq_len), F32),
        compiler_params=_params(("arbitrary",)),
        name="forget_suffix",
    )(logf_t)


def _split3(x):
    hi = x.astype(BF16).astype(F32)
    r = x - hi
    mid = r.astype(BF16).astype(F32)
    return hi, mid, r - mid


def _faug_kernel(f_ref, o_ref):
    f = f_ref[...]
    lane = _lane((f.shape[0], LANES))
    for p in range(H_FOX // 2):
        out = jnp.where((lane >= 2 * FSPLIT) & (lane < 3 * FSPLIT), 1.0, 0.0)
        for e in range(2):
            parts = _split3(f[:, 2 * p + e:2 * p + e + 1])
            for j in range(FSPLIT):
                out = jnp.where(lane == FSPLIT * e + j, parts[j], out)
        o_ref[p] = out.astype(BF16)


def forget_key_columns(f_col):
    t = f_col.shape[0]
    tm = _pick(t, (512, 256))
    npair = H_FOX // 2
    return pl.pallas_call(
        _faug_kernel,
        grid=(t // tm,),
        in_specs=[pl.BlockSpec((tm, HEAD_PAD), lambda i: (i, 0))],
        out_specs=pl.BlockSpec((npair, tm, LANES), lambda i: (0, i, 0)),
        out_shape=jax.ShapeDtypeStruct((npair, t, LANES), BF16),
        compiler_params=_params(("arbitrary",)),
        name="forget_key_columns",
    )(f_col)


SUBLANES = 8
FSPLIT = 3


def _fox_p_kernel(q_ref, kb_ref, vt_ref, fqr_ref, fa_ref, o_ref,
                  qt_ref, m_ref, l_ref, acc_ref, s4_ref, p2_ref, al_ref, *, tq, tk):
    qi = pl.program_id(1)
    ki = pl.program_id(2)
    npair = H_FOX // 2
    row_hi = lax.broadcasted_iota(jnp.int32, (LANES, tq), 0) >= DH
    nacc = 4

    @pl.when(ki == 0)
    def _():
        q = q_ref[...] * (DH ** -0.5)
        row = lax.broadcasted_iota(jnp.int32, (LANES, tq), 0)
        for p in range(npair):
            qt = q[:, p * LANES:(p + 1) * LANES].T
            for e in range(2):
                h = 2 * p + e
                qt_ref[h, 0:LANES, :] = jnp.where(row_hi == bool(e), qt, 0.0).astype(BF16)
                fq = _split3(fqr_ref[h:h + 1, :])
                ext = jnp.where((row >= FSPLIT * e) & (row < FSPLIT * (e + 1)), -1.0, 0.0)
                for j in range(FSPLIT):
                    ext = jnp.where(row == 2 * FSPLIT + j, fq[j], ext)
                qt_ref[h, LANES:2 * LANES, :] = ext.astype(BF16)
        m_ref[...] = jnp.full(m_ref.shape, NEG_INF, F32)
        l_ref[...] = jnp.zeros_like(l_ref)
        acc_ref[...] = jnp.zeros_like(acc_ref)

    def tree(op, xs):
        while len(xs) > 1:
            xs = [op(xs[i], xs[i + 1]) for i in range(0, len(xs) - 1, 2)] + ([xs[-1]] if len(xs) % 2 else [])
        return xs[0]

    def step(masked):
        def scores(p, slot):
            ka = jnp.concatenate([kb_ref[p], fa_ref[p]], axis=1)
            for e in range(2):
                s4_ref[slot + e] = _dot(ka, qt_ref[2 * p + e])

        scores(0, 0)

        def body(pp, carry):
            one_pair(2 * pp, 0)
            one_pair(2 * pp + 1, 2)
            return carry

        def one_pair(p, cur):
            vt = vt_ref[p]
            pvs = []
            scores(jnp.minimum(p + 1, npair - 1), 2 - cur)
            for e in range(2):
                h = 2 * p + e
                s_ref, p_ref = s4_ref.at[cur + e], p2_ref.at[e]
                m_all, l_all = m_ref[h], l_ref[h]
                m_news, l_news, alphas, nrows = [], [], [], []
                for c in range(tq // LANES):
                    cs = slice(c * LANES, (c + 1) * LANES)
                    nr = min(tk, LANES * (c + 1)) if masked else tk
                    nrows.append(nr)
                    mx = [jnp.full((SUBLANES, LANES), NEG_INF, F32)] * nacc
                    for r in range(nr // SUBLANES):
                        rs = slice(r * SUBLANES, (r + 1) * SUBLANES)
                        s = s_ref[rs, cs]
                        if masked and (r + 1) * SUBLANES > c * LANES:
                            key = lax.broadcasted_iota(jnp.int32, (SUBLANES, LANES), 0) + r * SUBLANES
                            qry = lax.broadcasted_iota(jnp.int32, (SUBLANES, LANES), 1) + c * LANES
                            s = jnp.where(key <= qry, s, NEG_INF)
                            s_ref[rs, cs] = s
                        mx[r % nacc] = jnp.maximum(mx[r % nacc], s)
                    m_prev = m_all[:, cs]
                    m_new = jnp.maximum(m_prev, jnp.max(tree(jnp.maximum, mx), axis=0, keepdims=True))
                    m_news.append(m_new)
                    alphas.append(jnp.exp(m_prev - m_new))
                for c in range(tq // LANES):
                    cs = slice(c * LANES, (c + 1) * LANES)
                    nr = nrows[c]
                    m16 = jnp.broadcast_to(m_news[c], (2 * SUBLANES, LANES))
                    sm = [jnp.zeros((2 * SUBLANES, LANES), F32)] * nacc
                    for r in range(nr // (2 * SUBLANES)):
                        rs = slice(r * 2 * SUBLANES, (r + 1) * 2 * SUBLANES)
                        pe = jnp.exp(s_ref[rs, cs] - m16)
                        sm[r % nacc] = sm[r % nacc] + pe
                        p_ref[rs, cs] = pe.astype(BF16)
                    if nr < tk:
                        p_ref[nr:tk, cs] = jnp.zeros((tk - nr, LANES), BF16)
                    l_news.append(alphas[c] * l_all[:, cs]
                                  + jnp.sum(tree(jnp.add, sm), axis=0, keepdims=True))
                m_ref[h] = jnp.concatenate(m_news, axis=1)
                l_ref[h] = jnp.concatenate(l_news, axis=1)
                al_ref[e] = jnp.concatenate(alphas, axis=1)
                pvs.append(_dot(vt, p_ref[...]))
            acc_ref[p] = (jnp.where(row_hi, al_ref[1], al_ref[0]) * acc_ref[p]
                          + jnp.where(row_hi, pvs[1], pvs[0]))

        lax.fori_loop(0, npair // 2, body, 0)

    @pl.when(ki < qi)
    def _():
        step(False)

    @pl.when(ki == qi)
    def _():
        step(True)
        for p in range(npair):
            denom = jnp.where(row_hi, l_ref[2 * p + 1], l_ref[2 * p])
            o_ref[:, p * LANES:(p + 1) * LANES] = (acc_ref[p] / denom).T.astype(o_ref.dtype)


def fox_prompt(z, kb, vbt, f_t, fa, *, seq_len):
    t = z.shape[0]
    nb = t // seq_len
    tq = tk = _pick(seq_len, (512, 256, 128))
    nq = seq_len // tq
    npair = H_FOX // 2
    return pl.pallas_call(
        functools.partial(_fox_p_kernel, tq=tq, tk=tk),
        grid=(nb, nq, nq),
        in_specs=[
            pl.BlockSpec((tq, D_FOX), lambda b, i, j: (b * nq + i, 0)),
            pl.BlockSpec((npair, tk, LANES), lambda b, i, j: (0, b * nq + jnp.minimum(i, j), 0)),
            pl.BlockSpec((None, npair, LANES, tk), lambda b, i, j: (b, 0, 0, jnp.minimum(i, j))),
            pl.BlockSpec((None, HEAD_PAD, tq), lambda b, i, j: (b, 0, i)),
            pl.BlockSpec((npair, tk, LANES), lambda b, i, j: (0, b * nq + jnp.minimum(i, j), 0)),
        ],
        out_specs=pl.BlockSpec((tq, D_FOX), lambda b, i, j: (b * nq + i, 0)),
        out_shape=jax.ShapeDtypeStruct((t, D_FOX), BF16),
        scratch_shapes=[
            pltpu.VMEM((H_FOX, 2 * LANES, tq), BF16),
            pltpu.VMEM((H_FOX, 1, tq), F32),
            pltpu.VMEM((H_FOX, 1, tq), F32),
            pltpu.VMEM((npair, LANES, tq), F32),
            pltpu.VMEM((4, tk, tq), F32),
            pltpu.VMEM((2, tk, tq), BF16),
            pltpu.VMEM((2, 1, tq), F32),
        ],
        compiler_params=_params(("arbitrary", "arbitrary", "arbitrary")),
        name="fox_prompt",
    )(z, kb, vbt, f_t, fa)


def _fox_s_kernel(pt_ref, q_ref, kn_ref, vn_ref, lfr_ref, lfn_ref, *rest, npg, nj, ls):
    lf_refs = rest[:npg]
    k_refs = rest[npg:2 * npg]
    v_refs = rest[2 * npg:3 * npg]
    o_ref, qbd_ref, fq_ref, carry_ref, m_ref, l_ref, acc_ref = rest[3 * npg:]
    j = pl.program_id(1)
    nrow = H_FOX * ls

    def update(s_raw, fk, mask, vs, v_transposed):
        fq = fq_ref[...]
        s = jnp.concatenate(
            [s_raw[h * ls:(h + 1) * ls, :] + fq[h * ls:(h + 1) * ls, :] - fk[h:h + 1, :]
             for h in range(H_FOX)], axis=0)
        if mask is not None:
            s = jnp.where(mask, s, NEG_INF)
        m_prev = m_ref[...]
        m_new = jnp.maximum(m_prev, jnp.max(s, axis=-1, keepdims=True))
        alpha = jnp.exp(m_prev - m_new)
        pe = jnp.exp(s - m_new)
        l_ref[...] = alpha * l_ref[...] + jnp.sum(pe, axis=-1, keepdims=True)
        m_ref[...] = m_new
        pb = pe.astype(BF16)
        mm = _dot_nt if v_transposed else _dot
        pv = mm(pb[:, :LANES], vs[0])
        for u in range(1, len(vs)):
            pv = pv + mm(pb[:, u * LANES:(u + 1) * LANES], vs[u])
        acc_ref[...] = alpha * acc_ref[...] + pv

    @pl.when(j == 0)
    def _():
        q = q_ref[...] * (DH ** -0.5)
        head = _lane(q.shape) >> 6
        qbd_ref[...] = jnp.concatenate(
            [jnp.where(head == h, q, 0.0) for h in range(H_FOX)], axis=0).astype(BF16)
        m_ref[...] = jnp.full(m_ref.shape, NEG_INF, F32)
        l_ref[...] = jnp.zeros_like(l_ref)
        acc_ref[...] = jnp.zeros_like(acc_ref)
        lfr = lfr_ref[...]
        fq_rows = -(_suffix_scan_rows(lfr) - lfr)
        fq_ref[...] = jnp.concatenate([fq_rows[:, h:h + 1] for h in range(H_FOX)], axis=0)
        lfn = lfn_ref[...]
        inc = _suffix_scan(lfn)
        carry_ref[...] = jnp.broadcast_to(inc[:, 0:1], carry_ref.shape)
        kn = _pad_rows(kn_ref[...], LANES).astype(BF16)
        vn = _pad_rows(vn_ref[...], LANES).astype(BF16)
        col = _lane((nrow, LANES))
        qrow = lax.broadcasted_iota(jnp.int32, (nrow, LANES), 0) & (ls - 1)
        update(_dot_nt(qbd_ref[...], kn), -(inc - lfn), col <= qrow, [vn], False)

    carry = carry_ref[...]
    fks = []
    for u in range(npg):
        x = lf_refs[u][...]
        inc = _suffix_scan(x)
        fks.append(-(carry + (inc - x)))
        carry = carry + inc[:, 0:1]
    carry_ref[...] = carry
    qbd = qbd_ref[...]
    s_raw = jnp.concatenate([_dot(qbd, k_refs[u][...].astype(BF16)) for u in range(npg)], axis=1)
    update(s_raw, jnp.concatenate(fks, axis=1), None,
           [v_refs[u][...].astype(BF16) for u in range(npg)], True)

    @pl.when(j == nj - 1)
    def _():
        acc = acc_ref[...] / l_ref[...]
        head = _lane((ls, D_FOX)) >> 6
        out = jnp.zeros((ls, D_FOX), F32)
        for h in range(H_FOX):
            out = out + jnp.where(head == h, acc[h * ls:(h + 1) * ls, :], 0.0)
        o_ref[...] = out


def fox_sample(z, k_new, v_new, lf_rows, lf_new_t, cache_lf_t, cache_kt, cache_vt, page_table, *, seq_len):
    t = z.shape[0]
    ls = seq_len
    nb = t // ls
    n_pages = page_table.shape[1]
    npg = _pick(n_pages, (16, 8, 4, 2))
    nj = n_pages // npg
    page = cache_kt.shape[2]
    nrow = H_FOX * ls

    def page_spec(u, rows):
        return pl.BlockSpec((None, rows, page),
                            lambda b, j, pt: (pt[b, n_pages - 1 - (j * npg + u)], 0, 0))

    row_spec = pl.BlockSpec((ls, D_FOX), lambda b, j, pt: (b, 0))
    return pl.pallas_call(
        functools.partial(_fox_s_kernel, npg=npg, nj=nj, ls=ls),
        grid_spec=pltpu.PrefetchScalarGridSpec(
            num_scalar_prefetch=1,
            grid=(nb, nj),
            in_specs=[
                row_spec, row_spec, row_spec,
                pl.BlockSpec((ls, LANES), lambda b, j, pt: (b, 0)),
                pl.BlockSpec((None, HEAD_PAD, LANES), lambda b, j, pt: (b, 0, 0)),
            ] + [page_spec(u, HEAD_PAD) for u in range(npg)]
              + [page_spec(u, D_FOX) for u in range(npg)] * 2,
            out_specs=row_spec,
            scratch_shapes=[
                pltpu.VMEM((nrow, D_FOX), BF16),
                pltpu.VMEM((nrow, 1), F32),
                pltpu.VMEM((HEAD_PAD, LANES), F32),
                pltpu.VMEM((nrow, 1), F32),
                pltpu.VMEM((nrow, 1), F32),
                pltpu.VMEM((nrow, D_FOX), F32),
            ]),
        out_shape=jax.ShapeDtypeStruct((t, D_FOX), F32),
        compiler_params=_params(("arbitrary", "arbitrary")),
        name="fox_sample",
    )(page_table, z, k_new, v_new, lf_rows, lf_new_t,
      *([cache_lf_t] * npg), *([cache_kt] * npg), *([cache_vt] * npg))


def _heads_to_sublanes(logf, nb, seq_len):
    x = logf[:, :H_FOX].reshape(nb, seq_len, H_FOX).transpose(0, 2, 1)
    return jnp.pad(x, ((0, 0), (0, HEAD_PAD - H_FOX), (0, 0)))


def _trunk(x, pos, mem_k, mem_v, ret_state, conv_state, past, p):
    nb, seq_len, d = x.shape
    t = nb * seq_len
    sample = past is not None
    cat_dtype = F32 if sample else BF16
    h = x.reshape(t, d)
    new_ret, new_conv = [], []
    k_sh = v_sh = logf_sh = None
    for layer in range(DEPTH):
        if layer < N_A:
            z = norm_matmul(h, p['g_mix_pre'][layer], p['w_in_a'], layer)
            s0 = (jnp.zeros((nb, H_RET, DK_RET, DV_RET), F32) if ret_state is None else ret_state[layer])
            r, s_new = retention(z, s0, pos, seq_len=seq_len, out_dtype=cat_dtype)
            new_ret.append(s_new)
            m = mem_attend(z, (2 * D_RET_QK + 2 * D_RET_V) // D_MEMQ, mem_k[layer], mem_v[layer],
                           seq_len=seq_len, out_dtype=cat_dtype)
            mixed, w_out, w_layer = r, p['w_out_a'], layer
        else:
            if layer == N_A:
                k_sh, v_sh, lf, *attn = shared_kv(h, p['g_kv'], p['wk'], p['wv'], p['wf'], p['bf'],
                                                  seq_len=seq_len, attn_copies=not sample)
                logf_sh = lf[:, :H_FOX]
                lf_t = _heads_to_sublanes(lf, nb, seq_len)
                if not sample:
                    kb, vbt = attn
                    f_t = forget_suffix(lf_t)
                    fa = forget_key_columns(f_t.transpose(0, 2, 1).reshape(t, HEAD_PAD))
                else:
                    cache_kt, cache_vt, cache_lf_t, page_table = past
                    lf_new_t = jnp.pad(lf_t, ((0, 0), (0, 0), (0, LANES - seq_len)))
            j = layer - N_A
            z = norm_matmul(h, p['g_mix_pre'][layer], p['w_in_b'], j)
            if not sample:
                o = fox_prompt(z, kb, vbt, f_t, fa, seq_len=seq_len)
            else:
                o = fox_sample(z, k_sh, v_sh, lf, lf_new_t, cache_lf_t, cache_kt, cache_vt, page_table,
                               seq_len=seq_len)
            m = mem_attend(z, D_FOX // D_MEMQ, mem_k[layer], mem_v[layer],
                           seq_len=seq_len, out_dtype=cat_dtype)
            mixed, w_out, w_layer = o, p['w_out_b'], j
        h, cs = mix_out_conv_ffn(mixed, m, w_out, w_layer, p['g_mix_post'][layer], h, layer, p,
                                 seq_len=seq_len,
                                 conv_state=None if conv_state is None else conv_state[layer])
        new_conv.append(cs)
    y = h.reshape(nb, seq_len, d)
    k_o = k_sh.reshape(nb, seq_len, H_FOX, DH)
    v_o = v_sh.reshape(nb, seq_len, H_FOX, DH)
    logf_o = logf_sh.reshape(nb, seq_len, H_FOX)
    return y, k_o, v_o, logf_o, jnp.stack(new_ret), jnp.stack(new_conv)


def kernel(x_prompt, x_sample, mem_prompt, cache_k, cache_v, cache_logf, page_table, cache_mem_k, cache_mem_v, state_ret, state_conv, g_mix_pre, g_mix_post, g_ffn_pre, g_ffn_post, w_in_a, w_out_a, w_in_b, w_out_b, w_mem_kv, g_kv, w_kv_shared, b_f, w_ffn_up, conv_w, conv_b, w_ffn_down):
    wf = jnp.pad(w_kv_shared[:, 2 * D_FOX:], ((0, 0), (0, LANES - H_FOX))).astype(BF16)
    bf = jnp.pad(b_f, (0, LANES - H_FOX)).reshape(1, LANES)
    p = {'g_mix_pre': g_mix_pre, 'g_mix_post': g_mix_post, 'g_ffn_pre': g_ffn_pre, 'g_ffn_post': g_ffn_post,
         'w_in_a': w_in_a.astype(BF16), 'w_out_a': w_out_a.astype(BF16),
         'w_in_b': w_in_b.astype(BF16), 'w_out_b': w_out_b.astype(BF16),
         'g_kv': g_kv, 'wk': w_kv_shared[:, :D_FOX].astype(BF16),
         'wv': w_kv_shared[:, D_FOX:2 * D_FOX].astype(BF16), 'wf': wf, 'bf': bf,
         'w_ffn_up': w_ffn_up.astype(BF16), 'conv_w': conv_w, 'conv_b': conv_b,
         'w_ffn_down': w_ffn_down.astype(BF16)}

    nb, seq_len, d = x_prompt.shape
    n_mem = mem_prompt.shape[1]
    mk, mv = mem_kv(mem_prompt.reshape(nb * n_mem, d), w_mem_kv.astype(BF16))
    mk = mk.reshape(DEPTH, nb, n_mem, D_MEMQ)
    mv = mv.reshape(DEPTH, nb, n_mem, D_MEMQ)
    pos_p = jnp.arange(seq_len, dtype=jnp.int32)
    y_p, k_p, v_p, logf_p, ret_p, conv_p = _trunk(x_prompt, pos_p, mk, mv, None, None, None, p)
    mem_k_p = mk.reshape(DEPTH, nb, n_mem, H_MEM, DH)
    mem_v_p = mv.reshape(DEPTH, nb, n_mem, H_MEM, DH)

    db, ls, _ = x_sample.shape
    n_phys, page = cache_k.shape[:2]
    past_len = page_table.shape[1] * page
    pos_s = past_len + jnp.arange(ls, dtype=jnp.int32)
    cache_lf_t = jnp.pad(cache_logf.transpose(0, 2, 1), ((0, 0), (0, HEAD_PAD - H_FOX), (0, 0)))
    cache_kt = cache_k.transpose(0, 2, 3, 1).reshape(n_phys, D_FOX, page)
    cache_vt = cache_v.transpose(0, 2, 3, 1).reshape(n_phys, D_FOX, page)
    past = (cache_kt, cache_vt, cache_lf_t, page_table)
    y_s, k_s, v_s, logf_s, ret_s, conv_s = _trunk(
        x_sample, pos_s, cache_mem_k.reshape(DEPTH, db, N_MEM, D_MEMQ),
        cache_mem_v.reshape(DEPTH, db, N_MEM, D_MEMQ), state_ret, state_conv, past, p)

    return (y_p, y_s, k_p, v_p, logf_p, k_s, v_s, logf_s,
            mem_k_p, mem_v_p, ret_p, ret_s, conv_p, conv_s)
```

```python
import functools

import jax
import jax.numpy as jnp
from jax import lax
from jax.experimental import pallas as pl
from jax.experimental.pallas import tpu as pltpu

F32 = jnp.float32
BF16 = jnp.bfloat16

D_MODEL = 1024
DEPTH = 4
N_A = 2
H_RET = 6
DK_RET = 64
DV_RET = 128
H_FOX = 12
DH = 64
H_MEM = 4
N_MEM = 256
D_RET_QK = H_RET * DK_RET
D_RET_V = H_RET * DV_RET
D_FOX = H_FOX * DH
D_MEMQ = H_MEM * DH
D_FF = 2816
RET_CHUNK = 128
ROPE_BASE = 10000.0
EPS = 1e-6

LANES = 128
HEAD_PAD = 16
VMEM_LIMIT = 48 * 1024 * 1024

NEG_INF = float("-inf")


def _params(sem):
    return pltpu.CompilerParams(dimension_semantics=sem, vmem_limit_bytes=VMEM_LIMIT)


def _rms(x, g):
    return x * lax.rsqrt(jnp.mean(x * x, axis=-1, keepdims=True) + EPS) * g


def _dot(a, b):
    return jnp.dot(a, b, preferred_element_type=F32)


def _dot_nt(a, b):
    return lax.dot_general(a, b, (((1,), (1,)), ((), ())), preferred_element_type=F32)


def _lane(shape):
    return lax.broadcasted_iota(jnp.int32, shape, len(shape) - 1)


def _pad_rows(x, n):
    if x.shape[0] == n:
        return x
    return jnp.concatenate([x, jnp.zeros((n - x.shape[0],) + x.shape[1:], x.dtype)], axis=0)


def _pick(n, cands):
    for c in cands:
        if n % c == 0:
            return c
    return n


NM_WEIGHT_BYTES = 6 * 1024 * 1024


def _nm_kernel(x_ref, g_ref, w_ref, o_ref, xn_ref, *, norm):
    @pl.when(pl.program_id(1) == 0)
    def _():
        x = x_ref[...]
        if norm:
            x = _rms(x, g_ref[...])
        xn_ref[...] = x.astype(BF16)

    o_ref[...] = _dot(xn_ref[...], w_ref[...])


def norm_matmul(x, g, w, layer, *, norm=True):
    t, d = x.shape
    n = w.shape[-1]
    tm = _pick(t, (512, 256))
    tn = n if d * n * 2 <= NM_WEIGHT_BYTES else _pick(n, (512, 256, 128))
    return pl.pallas_call(
        functools.partial(_nm_kernel, norm=norm),
        grid=(t // tm, n // tn),
        in_specs=[
            pl.BlockSpec((tm, d), lambda i, j: (i, 0)),
            pl.BlockSpec((1, d), lambda i, j: (0, 0)),
            pl.BlockSpec((None, d, tn), lambda i, j: (layer, 0, j)),
        ],
        out_specs=pl.BlockSpec((tm, tn), lambda i, j: (i, j)),
        out_shape=jax.ShapeDtypeStruct((t, n), F32),
        scratch_shapes=[pltpu.VMEM((tm, d), BF16)],
        compiler_params=_params(("arbitrary", "arbitrary")),
        name="norm_matmul",
    )(x, g.reshape(1, d), w)


def _memkv_kernel(x_ref, w_ref, k_ref, v_ref):
    acc = _dot(x_ref[...].astype(BF16), w_ref[...])
    k_ref[...] = acc[:, :D_MEMQ]
    v_ref[...] = acc[:, D_MEMQ:]


def mem_kv(x, w):
    t, d = x.shape
    tm = _pick(t, (1024, 512, 256))
    out = jax.ShapeDtypeStruct((DEPTH, t, D_MEMQ), F32)
    return pl.pallas_call(
        _memkv_kernel,
        grid=(t // tm, DEPTH),
        in_specs=[
            pl.BlockSpec((tm, d), lambda i, l: (i, 0)),
            pl.BlockSpec((None, d, 2 * D_MEMQ), lambda i, l: (l, 0, 0)),
        ],
        out_specs=[pl.BlockSpec((None, tm, D_MEMQ), lambda i, l: (l, i, 0))] * 2,
        out_shape=[out, out],
        compiler_params=_params(("arbitrary", "arbitrary")),
        name="mem_kv",
    )(x, w)


def _log_sigmoid(x):
    return jnp.minimum(x, 0.0) - jnp.log1p(jnp.exp(-jnp.abs(x)))


def _kv_kernel(x_ref, g_ref, wk_ref, wv_ref, wf_ref, bf_ref, k_ref, v_ref, lf_ref, *attn_refs):
    xn = _rms(x_ref[...], g_ref[...]).astype(BF16)
    k = _dot(xn, wk_ref[...])
    v = _dot(xn, wv_ref[...])
    k_ref[...] = k
    v_ref[...] = v
    lf_ref[...] = _log_sigmoid(_dot(xn, wf_ref[...]) + bf_ref[...])
    if attn_refs:
        kb_ref, vbt_ref = attn_refs
        for p in range(D_FOX // LANES):
            kb_ref[p] = k[:, p * LANES:(p + 1) * LANES].astype(BF16)
            vbt_ref[p] = v[:, p * LANES:(p + 1) * LANES].T.astype(BF16)


def shared_kv(h, g, wk, wv, wf, bf, *, seq_len, attn_copies):
    t, d = h.shape
    tm = _pick(t, (512, 256))
    npair = D_FOX // LANES
    full = lambda shape: pl.BlockSpec(shape, lambda i: (0,) * len(shape))
    out_specs = [
        pl.BlockSpec((tm, D_FOX), lambda i: (i, 0)),
        pl.BlockSpec((tm, D_FOX), lambda i: (i, 0)),
        pl.BlockSpec((tm, LANES), lambda i: (i, 0)),
    ]
    out_shape = [
        jax.ShapeDtypeStruct((t, D_FOX), F32),
        jax.ShapeDtypeStruct((t, D_FOX), F32),
        jax.ShapeDtypeStruct((t, LANES), F32),
    ]
    if attn_copies:
        tpb = seq_len // tm
        out_specs += [
            pl.BlockSpec((npair, tm, LANES), lambda i: (0, i, 0)),
            pl.BlockSpec((None, npair, LANES, tm), lambda i: (i // tpb, 0, 0, i % tpb)),
        ]
        out_shape += [
            jax.ShapeDtypeStruct((npair, t, LANES), BF16),
            jax.ShapeDtypeStruct((t // seq_len, npair, LANES, seq_len), BF16),
        ]
    return pl.pallas_call(
        _kv_kernel,
        grid=(t // tm,),
        in_specs=[
            pl.BlockSpec((tm, d), lambda i: (i, 0)),
            full((1, d)), full((d, D_FOX)), full((d, D_FOX)), full((d, LANES)), full((1, LANES)),
        ],
        out_specs=out_specs,
        out_shape=out_shape,
        compiler_params=_params(("arbitrary",)),
        name="shared_kv",
    )(h, g.reshape(1, d), wk, wv, wf, bf)


FFN_TILE = 512
FFN_ROWS = 16
FFN_CHUNK = 256
FFN_VMEM_LIMIT = 56 * 1024 * 1024


def _gelu_tanh(x):
    return 0.5 * x * (1.0 + jnp.tanh(0.7978845608028654 * (x + 0.044715 * (x * x * x))))


def _ffn_kernel(*refs, tm, tf, tiles_per_batch, seq_rows):
    (c1_ref, c2_ref, wo_ref, gmix_ref,
     h_ref, gpre_ref, wg_ref, wv_ref, cwg_ref, cwv_ref, cbg_ref, cbv_ref, wd_ref, gpost_ref) = refs[:14]
    if seq_rows is None:
        o_ref, tg_ref, tv_ref, hm_ref, xn_ref, ug_ref, uv_ref, hid_ref, cg_ref, cv_ref = refs[14:]
        p1g_ref = p1v_ref = p2g_ref = p2v_ref = None
    else:
        (p1g_ref, p1v_ref, p2g_ref, p2v_ref, o_ref, tg_ref, tv_ref,
         hm_ref, xn_ref, ug_ref, uv_ref, hid_ref) = refs[14:]
        cg_ref = cv_ref = None
    i = pl.program_id(0)
    rg = FFN_ROWS
    ch = FFN_CHUNK

    for u_ref, c_ref in ((ug_ref, cg_ref), (uv_ref, cv_ref)):
        if seq_rows is None:
            @pl.when((i % tiles_per_batch) == 0)
            def _():
                c_ref[...] = jnp.zeros((8, tf), F32)

            u_ref[0:8, :] = c_ref[...]
        else:
            u_ref[0:8, :] = jnp.zeros((8, tf), F32)

    if seq_rows is not None:
        r = lax.broadcasted_iota(jnp.int32, (rg, LANES), 0) & (seq_rows - 1)
        keep1 = r >= 1
        keep2 = r >= 2

    def pre(c):
        rows = slice(c * ch, (c + 1) * ch)
        n1 = c1_ref.shape[1]
        mix = (_dot(c1_ref[rows, :].astype(BF16), wo_ref[:n1, :])
               + _dot(c2_ref[rows, :].astype(BF16), wo_ref[n1:, :]))
        hm = h_ref[rows, :] + _rms(mix, gmix_ref[...])
        hm_ref[rows, :] = hm
        xn_ref[rows, :] = _rms(hm, gpre_ref[...]).astype(BF16)

    def up(c):
        pre(c)
        xc = xn_ref[c * ch:(c + 1) * ch, :]
        ug_ref[8 + c * ch:8 + (c + 1) * ch, :] = _dot(xc, wg_ref[...])
        uv_ref[8 + c * ch:8 + (c + 1) * ch, :] = _dot(xc, wv_ref[...])

    def conv(u_ref, cw_ref, cb_ref, p1_ref, p2_ref, r0, sl):
        a = u_ref[r0:r0 + rg + 8, sl]
        u = a[8:]
        u1 = pltpu.roll(a, 1, 0)[8:]
        u2 = pltpu.roll(a, 2, 0)[8:]
        if seq_rows is not None:
            u1 = jnp.where(keep1, u1, p1_ref[r0:r0 + rg, sl])
            u2 = jnp.where(keep2, u2, p2_ref[r0:r0 + rg, sl])
        return cb_ref[:, sl] + cw_ref[0:1, sl] * u2 + cw_ref[1:2, sl] * u1 + cw_ref[2:3, sl] * u

    def gate_mul(c):
        for lt in range(tf // LANES):
            sl = slice(lt * LANES, (lt + 1) * LANES)
            for r0 in range(c * ch, (c + 1) * ch, rg):
                gate = conv(ug_ref, cwg_ref, cbg_ref, p1g_ref, p2g_ref, r0, sl)
                val = conv(uv_ref, cwv_ref, cbv_ref, p1v_ref, p2v_ref, r0, sl)
                hid_ref[r0:r0 + rg, sl] = (_gelu_tanh(gate) * val).astype(BF16)

    def down(c):
        rows = slice(c * ch, (c + 1) * ch)
        o_ref[rows, :] = hm_ref[rows, :] + _rms(_dot(hid_ref[rows, :], wd_ref[...]), gpost_ref[...])

    nch = tm // ch
    ahead = 2
    for c in range(min(ahead, nch)):
        up(c)
    for c in range(nch):
        if c + ahead < nch:
            up(c + ahead)
        gate_mul(c)
        down(c)

    if seq_rows is None:
        cg_ref[...] = ug_ref[tm:tm + 8, :]
        cv_ref[...] = uv_ref[tm:tm + 8, :]
        tg_ref[0] = ug_ref[tm:tm + 8, :]
        tv_ref[0] = uv_ref[tm:tm + 8, :]
    else:
        tg_ref[...] = ug_ref[8:tm + 8, :]
        tv_ref[...] = uv_ref[8:tm + 8, :]


def mix_out_conv_ffn(c1, c2, w_out, w_layer, g_mix, h, layer, p, *, seq_len, conv_state=None):
    t, d = h.shape
    n1, n2 = c1.shape[1], c2.shape[1]
    nb = t // seq_len
    tf = D_FF
    w_up, cw, cb, w_down = p['w_ffn_up'], p['conv_w'], p['conv_b'], p['w_ffn_down']
    cb3 = cb.reshape(DEPTH, 1, 2 * D_FF)
    if conv_state is None:
        tm = _pick(seq_len, (FFN_TILE, FFN_CHUNK))
        tiles_per_batch = seq_len // tm
        seq_rows = None
        tail_shape = jax.ShapeDtypeStruct((t // tm, 8, D_FF), F32)
        tail_spec = pl.BlockSpec((1, 8, tf), lambda i: (i, 0, 0))
        extra_in, extra_specs = [], []
        extra_scratch = [pltpu.VMEM((8, tf), F32), pltpu.VMEM((8, tf), F32)]
    else:
        tm = _pick(t, (FFN_TILE, FFN_CHUNK))
        tiles_per_batch = 1
        seq_rows = seq_len
        tail_shape = jax.ShapeDtypeStruct((t, D_FF), F32)
        tail_spec = pl.BlockSpec((tm, tf), lambda i: (i, 0))
        st = conv_state
        z = jnp.zeros((nb, seq_len, 2 * D_FF), F32)
        prev1 = z.at[:, 0].set(st[:, 1]).reshape(t, 2 * D_FF)
        prev2 = z.at[:, 0].set(st[:, 0]).at[:, 1].set(st[:, 1]).reshape(t, 2 * D_FF)
        extra_in = [prev1, prev1, prev2, prev2]
        gspec = pl.BlockSpec((tm, tf), lambda i: (i, 0))
        vspec = pl.BlockSpec((tm, tf), lambda i: (i, 1))
        extra_specs = [gspec, vspec, gspec, vspec]
        extra_scratch = []
    kern = functools.partial(_ffn_kernel, tm=tm, tf=tf, tiles_per_batch=tiles_per_batch, seq_rows=seq_rows)
    once = pl.Buffered(1)
    h_new, tail_g, tail_v = pl.pallas_call(
        kern,
        grid=(t // tm,),
        in_specs=[
            pl.BlockSpec((tm, n1), lambda i: (i, 0)),
            pl.BlockSpec((tm, n2), lambda i: (i, 0)),
            pl.BlockSpec((None, n1 + n2, d), lambda i: (w_layer, 0, 0), pipeline_mode=once),
            pl.BlockSpec((1, d), lambda i: (0, 0)),
            pl.BlockSpec((tm, d), lambda i: (i, 0)),
            pl.BlockSpec((1, d), lambda i: (0, 0)),
            pl.BlockSpec((None, d, tf), lambda i: (layer, 0, 0), pipeline_mode=once),
            pl.BlockSpec((None, d, tf), lambda i: (layer, 0, 1), pipeline_mode=once),
            pl.BlockSpec((None, 3, tf), lambda i: (layer, 0, 0)),
            pl.BlockSpec((None, 3, tf), lambda i: (layer, 0, 1)),
            pl.BlockSpec((None, 1, tf), lambda i: (layer, 0, 0)),
            pl.BlockSpec((None, 1, tf), lambda i: (layer, 0, 1)),
            pl.BlockSpec((None, tf, d), lambda i: (layer, 0, 0), pipeline_mode=once),
            pl.BlockSpec((1, d), lambda i: (0, 0)),
        ] + extra_specs,
        out_specs=[pl.BlockSpec((tm, d), lambda i: (i, 0)), tail_spec, tail_spec],
        out_shape=[jax.ShapeDtypeStruct((t, d), F32), tail_shape, tail_shape],
        scratch_shapes=[pltpu.VMEM((tm, d), F32), pltpu.VMEM((tm, d), BF16),
                        pltpu.VMEM((tm + 8, tf), F32), pltpu.VMEM((tm + 8, tf), F32),
                        pltpu.VMEM((tm, tf), BF16)] + extra_scratch,
        compiler_params=pltpu.CompilerParams(dimension_semantics=("arbitrary",),
                                             vmem_limit_bytes=FFN_VMEM_LIMIT),
        name="conv_ffn",
    )(c1, c2, w_out, g_mix.reshape(1, d), h, p['g_ffn_pre'][layer].reshape(1, d), w_up, w_up, cw, cw,
      cb3, cb3, w_down, p['g_ffn_post'][layer].reshape(1, d), *extra_in)
    if conv_state is None:
        last = slice(tiles_per_batch - 1, None, tiles_per_batch)
        new_state = jnp.concatenate([tail_g[last, 6:8], tail_v[last, 6:8]], axis=-1)
    else:
        new_state = jnp.concatenate([tail_g.reshape(nb, seq_len, D_FF)[:, seq_len - 2:],
                                     tail_v.reshape(nb, seq_len, D_FF)[:, seq_len - 2:]], axis=-1)
    return h_new, new_state


def _rotary(x, cos, sins):
    parts = []
    for p in range(x.shape[1] // LANES):
        xb = x[:, p * LANES:(p + 1) * LANES]
        first = (_lane(xb.shape) & (DK_RET // 2)) == 0
        sw = jnp.where(first, pltpu.roll(xb, LANES - DK_RET // 2, 1), pltpu.roll(xb, DK_RET // 2, 1))
        parts.append(xb * cos[:, p * LANES:(p + 1) * LANES] + sw * sins[:, p * LANES:(p + 1) * LANES])
    return jnp.concatenate(parts, axis=1)


def _ret_kernel(q_ref, k_ref, v_ref, g_ref, cos_ref, sin_ref, dm_ref, qd_ref, kd_ref, sd_ref, s0_ref,
                r_ref, so_ref, s_ref, *, tq, nq):
    c = RET_CHUNK
    qi = pl.program_id(1)
    zero_half = jnp.zeros((DK_RET, DV_RET), F32)

    @pl.when(qi == 0)
    def _():
        for h in range(H_RET):
            halves = [zero_half, zero_half]
            halves[h % 2] = s0_ref[h]
            s_ref[h] = jnp.concatenate(halves, axis=0)

    cos = cos_ref[...]
    sins = sin_ref[...]
    q = _rotary(q_ref[...], cos, sins)
    k = _rotary(k_ref[...], cos, sins) * (DK_RET ** -0.5)
    rows = max(tq, c)
    for cs in range(0, rows, c):
        n = min(c, tq)
        qc = _pad_rows(q[cs:cs + n], c)
        kc = _pad_rows(k[cs:cs + n], c)
        vc = _pad_rows(v_ref[cs:cs + n, :], c)
        qd = qc * qd_ref[...]
        kd = kc * kd_ref[...]
        for p in range(H_RET // 2):
            sl = slice(p * LANES, (p + 1) * LANES)
            q2, k2, qd2, kd2 = qc[:, sl], kc[:, sl], qd[:, sl], kd[:, sl]
            k2b = k2.astype(BF16)
            hi = _lane(q2.shape) >= DK_RET
            for e in range(2):
                h = 2 * p + e
                mine = hi if e else jnp.logical_not(hi)
                qm = jnp.where(mine, q2, 0.0).astype(BF16)
                qdm = jnp.where(mine, qd2, 0.0).astype(BF16)
                kdm = jnp.where(mine, kd2, 0.0)
                vh = vc[:, h * DV_RET:(h + 1) * DV_RET].astype(BF16)
                inner = (_dot_nt(qm, k2b) * dm_ref[h]).astype(BF16)
                s_old = s_ref[h]
                o = _dot(inner, vh) + _dot(qdm, s_old.astype(BF16))
                s_ref[h] = sd_ref[h] * s_old + _dot(kdm.T.astype(BF16), vh)
                cen = o - jnp.mean(o, axis=-1, keepdims=True)
                y = cen * lax.rsqrt(jnp.mean(cen * cen, axis=-1, keepdims=True) + EPS)
                gh = g_ref[cs:cs + n, h * DV_RET:(h + 1) * DV_RET]
                r = (gh / (1.0 + jnp.exp(-gh))) * y[:n]
                r_ref[cs:cs + n, h * DV_RET:(h + 1) * DV_RET] = r.astype(r_ref.dtype)

    @pl.when(qi == nq - 1)
    def _():
        for h in range(H_RET):
            e = h % 2
            so_ref[h] = s_ref[h][e * DK_RET:(e + 1) * DK_RET, :]


def _ret_tables(chunk):
    c = RET_CHUNK
    log_gamma = jnp.log1p(-(2.0 ** (-5.0 - jnp.arange(H_RET, dtype=F32))))
    idx = jnp.arange(c, dtype=F32)
    rel = idx[:, None] - idx[None, :]
    dmask = jnp.where(rel >= 0, jnp.exp(log_gamma[:, None, None] * jnp.maximum(rel, 0.0)), 0.0)
    qdec = jnp.exp(log_gamma[None, :] * (idx[:, None] + 1.0))
    kdec = jnp.exp(log_gamma[None, :] * (chunk - 1.0 - idx[:, None]))
    sdec = jnp.exp(log_gamma * chunk)
    qdec = jnp.repeat(qdec, DK_RET, axis=1)
    kdec = jnp.repeat(kdec, DK_RET, axis=1)
    sdec = jnp.broadcast_to(sdec[:, None, None], (H_RET, 1, DV_RET))
    return dmask, qdec, kdec, sdec


def _rope_tables(pos):
    half = DK_RET // 2
    inv = ROPE_BASE ** (-jnp.arange(half, dtype=F32) / half)
    ang = pos.astype(F32)[:, None] * inv[None, :]
    cos, sin = jnp.cos(ang), jnp.sin(ang)
    cos_t = jnp.tile(jnp.concatenate([cos, cos], axis=1), (1, H_RET))
    sin_t = jnp.tile(jnp.concatenate([-sin, sin], axis=1), (1, H_RET))
    return cos_t, sin_t


def retention(z, s0, pos, *, seq_len, out_dtype):
    t = z.shape[0]
    nb = t // seq_len
    chunk = RET_CHUNK if seq_len % RET_CHUNK == 0 else seq_len
    tq = _pick(seq_len, (512, 256, 128))
    nq = seq_len // tq
    dmask, qdec, kdec, sdec = _ret_tables(chunk)
    cos_t, sin_t = _rope_tables(pos)
    nqk = D_RET_QK
    full = lambda shape: pl.BlockSpec(shape, lambda b, i: (0,) * len(shape))
    return pl.pallas_call(
        functools.partial(_ret_kernel, tq=tq, nq=nq),
        grid=(nb, nq),
        in_specs=[
            pl.BlockSpec((tq, nqk), lambda b, i: (b * nq + i, 0)),
            pl.BlockSpec((tq, nqk), lambda b, i: (b * nq + i, 1)),
            pl.BlockSpec((tq, D_RET_V), lambda b, i: (b * nq + i, 1)),
            pl.BlockSpec((tq, D_RET_V), lambda b, i: (b * nq + i, 2)),
            pl.BlockSpec((tq, nqk), lambda b, i: (i, 0)),
            pl.BlockSpec((tq, nqk), lambda b, i: (i, 0)),
            full((H_RET, RET_CHUNK, RET_CHUNK)),
            full((RET_CHUNK, nqk)),
            full((RET_CHUNK, nqk)),
            full((H_RET, 1, DV_RET)),
            pl.BlockSpec((None, H_RET, DK_RET, DV_RET), lambda b, i: (b, 0, 0, 0)),
        ],
        out_specs=[
            pl.BlockSpec((tq, D_RET_V), lambda b, i: (b * nq + i, 0)),
            pl.BlockSpec((None, H_RET, DK_RET, DV_RET), lambda b, i: (b, 0, 0, 0)),
        ],
        out_shape=[jax.ShapeDtypeStruct((t, D_RET_V), out_dtype),
                   jax.ShapeDtypeStruct((nb, H_RET, DK_RET, DV_RET), F32)],
        scratch_shapes=[pltpu.VMEM((H_RET, 2 * DK_RET, DV_RET), F32)],
        compiler_params=_params(("arbitrary", "arbitrary")),
        name="retention",
    )(z, z, z, z, cos_t, sin_t, dmask, qdec, kdec, sdec, s0)


def _mem_kernel(q_ref, mk_ref, mv_ref, o_ref, *, tq):
    rows = max(tq, 16)
    q = _pad_rows(q_ref[...] * (DH ** -0.5), rows)
    for p in range(D_MEMQ // LANES):
        sl = slice(p * LANES, (p + 1) * LANES)
        q2 = q[:, sl]
        k2 = mk_ref[:, sl].astype(BF16)
        v2 = mv_ref[:, sl].astype(BF16)
        hi = _lane(q2.shape) >= DH
        outs = []
        for e in range(2):
            mine = hi if e else jnp.logical_not(hi)
            s = _dot_nt(jnp.where(mine, q2, 0.0).astype(BF16), k2)
            pe = jnp.exp(s - jnp.max(s, axis=-1, keepdims=True))
            outs.append(_dot(pe.astype(BF16), v2) / jnp.sum(pe, axis=-1, keepdims=True))
        o = jnp.where(hi, outs[1], outs[0])
        o_ref[:, sl] = o[:tq].astype(o_ref.dtype)


def mem_attend(z, col_block, mk, mv, *, seq_len, out_dtype):
    t = z.shape[0]
    nb = t // seq_len
    tq = _pick(seq_len, (512, 256, 128))
    nq = seq_len // tq
    return pl.pallas_call(
        functools.partial(_mem_kernel, tq=tq),
        grid=(nb, nq),
        in_specs=[
            pl.BlockSpec((tq, D_MEMQ), lambda b, i: (b * nq + i, col_block)),
            pl.BlockSpec((None, N_MEM, D_MEMQ), lambda b, i: (b, 0, 0)),
            pl.BlockSpec((None, N_MEM, D_MEMQ), lambda b, i: (b, 0, 0)),
        ],
        out_specs=pl.BlockSpec((tq, D_MEMQ), lambda b, i: (b * nq + i, 0)),
        out_shape=jax.ShapeDtypeStruct((t, D_MEMQ), out_dtype),
        compiler_params=_params(("arbitrary", "arbitrary")),
        name="mem_attend",
    )(z, mk, mv)


def _suffix_scan(x):
    lane = _lane(x.shape)
    t = x
    d = 1
    while d < LANES:
        t = t + jnp.where(lane + d < LANES, pltpu.roll(t, LANES - d, 1), 0.0)
        d *= 2
    return t


def _suffix_scan_rows(x):
    n = x.shape[0]
    row = lax.broadcasted_iota(jnp.int32, x.shape, 0)
    t = x
    d = 1
    while d < n:
        t = t + jnp.where(row + d < n, pltpu.roll(t, n - d, 0), 0.0)
        d *= 2
    return t


def _cumsum_kernel(x_ref, fo_ref):
    carry = jnp.zeros((HEAD_PAD, LANES), F32)
    for p in range(x_ref.shape[1] // LANES - 1, -1, -1):
        sl = slice(p * LANES, (p + 1) * LANES)
        x = x_ref[:, sl]
        inc = _suffix_scan(x)
        fo_ref[:, sl] = -(carry + (inc - x))
        carry = carry + inc[:, 0:1]


def forget_suffix(logf_t):
    nb, _, seq_len = logf_t.shape
    spec = pl.BlockSpec((None, HEAD_PAD, seq_len), lambda b: (b, 0, 0))
    return pl.pallas_call(
        _cumsum_kernel,
        grid=(nb,),
        in_specs=[spec],
        out_specs=spec,
        out_shape=jax.ShapeDtypeStruct((nb, HEAD_PAD, seq_len), F32),
        compiler_params=_params(("arbitrary",)),
        name="forget_suffix",
    )(logf_t)


def _split3(x):
    hi = x.astype(BF16).astype(F32)
    r = x - hi
    mid = r.astype(BF16).astype(F32)
    return hi, mid, r - mid


def _faug_kernel(f_ref, o_ref):
    f = f_ref[...]
    lane = _lane((f.shape[0], LANES))
    for p in range(H_FOX // 2):
        out = jnp.where((lane >= 2 * FSPLIT) & (lane < 3 * FSPLIT), 1.0, 0.0)
        for e in range(2):
            parts = _split3(f[:, 2 * p + e:2 * p + e + 1])
            for j in range(FSPLIT):
                out = jnp.where(lane == FSPLIT * e + j, parts[j], out)
        o_ref[p] = out.astype(BF16)


def forget_key_columns(f_col):
    t = f_col.shape[0]
    tm = _pick(t, (512, 256))
    npair = H_FOX // 2
    return pl.pallas_call(
        _faug_kernel,
        grid=(t // tm,),
        in_specs=[pl.BlockSpec((tm, HEAD_PAD), lambda i: (i, 0))],
        out_specs=pl.BlockSpec((npair, tm, LANES), lambda i: (0, i, 0)),
        out_shape=jax.ShapeDtypeStruct((npair, t, LANES), BF16),
        compiler_params=_params(("arbitrary",)),
        name="forget_key_columns",
    )(f_col)


SUBLANES = 8
FSPLIT = 3


def _fox_p_kernel(q_ref, kb_ref, vt_ref, fqr_ref, fa_ref, o_ref,
                  qt_ref, m_ref, l_ref, acc_ref, s4_ref, p2_ref, al_ref, *, tq, tk):
    qi = pl.program_id(1)
    ki = pl.program_id(2)
    npair = H_FOX // 2
    row_hi = lax.broadcasted_iota(jnp.int32, (LANES, tq), 0) >= DH
    nacc = 4

    @pl.when(ki == 0)
    def _():
        q = q_ref[...] * (DH ** -0.5)
        row = lax.broadcasted_iota(jnp.int32, (LANES, tq), 0)
        for p in range(npair):
            qt = q[:, p * LANES:(p + 1) * LANES].T
            for e in range(2):
                h = 2 * p + e
                qt_ref[h, 0:LANES, :] = jnp.where(row_hi == bool(e), qt, 0.0).astype(BF16)
                fq = _split3(fqr_ref[h:h + 1, :])
                ext = jnp.where((row >= FSPLIT * e) & (row < FSPLIT * (e + 1)), -1.0, 0.0)
                for j in range(FSPLIT):
                    ext = jnp.where(row == 2 * FSPLIT + j, fq[j], ext)
                qt_ref[h, LANES:2 * LANES, :] = ext.astype(BF16)
        m_ref[...] = jnp.full(m_ref.shape, NEG_INF, F32)
        l_ref[...] = jnp.zeros_like(l_ref)
        acc_ref[...] = jnp.zeros_like(acc_ref)

    def tree(op, xs):
        while len(xs) > 1:
            xs = [op(xs[i], xs[i + 1]) for i in range(0, len(xs) - 1, 2)] + ([xs[-1]] if len(xs) % 2 else [])
        return xs[0]

    def step(masked):
        def scores(p, slot):
            ka = jnp.concatenate([kb_ref[p], fa_ref[p]], axis=1)
            for e in range(2):
                s4_ref[slot + e] = _dot(ka, qt_ref[2 * p + e])

        scores(0, 0)

        def body(pp, carry):
            one_pair(2 * pp, 0)
            one_pair(2 * pp + 1, 2)
            return carry

        def one_pair(p, cur):
            vt = vt_ref[p]
            pvs = []
            scores(jnp.minimum(p + 1, npair - 1), 2 - cur)
            for e in range(2):
                h = 2 * p + e
                s_ref, p_ref = s4_ref.at[cur + e], p2_ref.at[e]
                m_all, l_all = m_ref[h], l_ref[h]
                m_news, l_news, alphas, nrows = [], [], [], []
                for c in range(tq // LANES):
                    cs = slice(c * LANES, (c + 1) * LANES)
                    nr = min(tk, LANES * (c + 1)) if masked else tk
                    nrows.append(nr)
                    mx = [jnp.full((SUBLANES, LANES), NEG_INF, F32)] * nacc
                    for r in range(nr // SUBLANES):
                        rs = slice(r * SUBLANES, (r + 1) * SUBLANES)
                        s = s_ref[rs, cs]
                        if masked and (r + 1) * SUBLANES > c * LANES:
                            key = lax.broadcasted_iota(jnp.int32, (SUBLANES, LANES), 0) + r * SUBLANES
                            qry = lax.broadcasted_iota(jnp.int32, (SUBLANES, LANES), 1) + c * LANES
                            s = jnp.where(key <= qry, s, NEG_INF)
                            s_ref[rs, cs] = s
                        mx[r % nacc] = jnp.maximum(mx[r % nacc], s)
                    m_prev = m_all[:, cs]
                    m_new = jnp.maximum(m_prev, jnp.max(tree(jnp.maximum, mx), axis=0, keepdims=True))
                    m_news.append(m_new)
                    alphas.append(jnp.exp(m_prev - m_new))
                for c in range(tq // LANES):
                    cs = slice(c * LANES, (c + 1) * LANES)
                    nr = nrows[c]
                    m16 = jnp.broadcast_to(m_news[c], (2 * SUBLANES, LANES))
                    sm = [jnp.zeros((2 * SUBLANES, LANES), F32)] * nacc
                    for r in range(nr // (2 * SUBLANES)):
                        rs = slice(r * 2 * SUBLANES, (r + 1) * 2 * SUBLANES)
                        pe = jnp.exp(s_ref[rs, cs] - m16)
                        sm[r % nacc] = sm[r % nacc] + pe
                        p_ref[rs, cs] = pe.astype(BF16)
                    if nr < tk:
                        p_ref[nr:tk, cs] = jnp.zeros((tk - nr, LANES), BF16)
                    l_news.append(alphas[c] * l_all[:, cs]
                                  + jnp.sum(tree(jnp.add, sm), axis=0, keepdims=True))
                m_ref[h] = jnp.concatenate(m_news, axis=1)
                l_ref[h] = jnp.concatenate(l_news, axis=1)
                al_ref[e] = jnp.concatenate(alphas, axis=1)
            pvs = [_dot(vt, p2_ref[e]) for e in range(2)]
            acc_ref[p] = (jnp.where(row_hi, al_ref[1], al_ref[0]) * acc_ref[p]
                          + jnp.where(row_hi, pvs[1], pvs[0]))

        lax.fori_loop(0, npair // 2, body, 0)

    @pl.when(ki < qi)
    def _():
        step(False)

    @pl.when(ki == qi)
    def _():
        step(True)
        for p in range(npair):
            denom = jnp.where(row_hi, l_ref[2 * p + 1], l_ref[2 * p])
            o_ref[:, p * LANES:(p + 1) * LANES] = (acc_ref[p] / denom).T.astype(o_ref.dtype)


def fox_prompt(z, kb, vbt, f_t, fa, *, seq_len):
    t = z.shape[0]
    nb = t // seq_len
    tq = tk = _pick(seq_len, (512, 256, 128))
    nq = seq_len // tq
    npair = H_FOX // 2
    return pl.pallas_call(
        functools.partial(_fox_p_kernel, tq=tq, tk=tk),
        grid=(nb, nq, nq),
        in_specs=[
            pl.BlockSpec((tq, D_FOX), lambda b, i, j: (b * nq + i, 0)),
            pl.BlockSpec((npair, tk, LANES), lambda b, i, j: (0, b * nq + jnp.minimum(i, j), 0)),
            pl.BlockSpec((None, npair, LANES, tk), lambda b, i, j: (b, 0, 0, jnp.minimum(i, j))),
            pl.BlockSpec((None, HEAD_PAD, tq), lambda b, i, j: (b, 0, i)),
            pl.BlockSpec((npair, tk, LANES), lambda b, i, j: (0, b * nq + jnp.minimum(i, j), 0)),
        ],
        out_specs=pl.BlockSpec((tq, D_FOX), lambda b, i, j: (b * nq + i, 0)),
        out_shape=jax.ShapeDtypeStruct((t, D_FOX), BF16),
        scratch_shapes=[
            pltpu.VMEM((H_FOX, 2 * LANES, tq), BF16),
            pltpu.VMEM((H_FOX, 1, tq), F32),
            pltpu.VMEM((H_FOX, 1, tq), F32),
            pltpu.VMEM((npair, LANES, tq), F32),
            pltpu.VMEM((4, tk, tq), F32),
            pltpu.VMEM((2, tk, tq), BF16),
            pltpu.VMEM((2, 1, tq), F32),
        ],
        compiler_params=_params(("arbitrary", "arbitrary", "arbitrary")),
        name="fox_prompt",
    )(z, kb, vbt, f_t, fa)


def _fox_s_kernel(pt_ref, q_ref, kn_ref, vn_ref, lfr_ref, lfn_ref, *rest, npg, nj, ls):
    lf_refs = rest[:npg]
    k_refs = rest[npg:2 * npg]
    v_refs = rest[2 * npg:3 * npg]
    o_ref, qbd_ref, fq_ref, carry_ref, m_ref, l_ref, acc_ref = rest[3 * npg:]
    j = pl.program_id(1)
    nrow = H_FOX * ls

    def update(s_raw, fk, mask, vs, v_transposed):
        fq = fq_ref[...]
        s = jnp.concatenate(
            [s_raw[h * ls:(h + 1) * ls, :] + fq[h * ls:(h + 1) * ls, :] - fk[h:h + 1, :]
             for h in range(H_FOX)], axis=0)
        if mask is not None:
            s = jnp.where(mask, s, NEG_INF)
        m_prev = m_ref[...]
        m_new = jnp.maximum(m_prev, jnp.max(s, axis=-1, keepdims=True))
        alpha = jnp.exp(m_prev - m_new)
        pe = jnp.exp(s - m_new)
        l_ref[...] = alpha * l_ref[...] + jnp.sum(pe, axis=-1, keepdims=True)
        m_ref[...] = m_new
        pb = pe.astype(BF16)
        mm = _dot_nt if v_transposed else _dot
        pv = mm(pb[:, :LANES], vs[0])
        for u in range(1, len(vs)):
            pv = pv + mm(pb[:, u * LANES:(u + 1) * LANES], vs[u])
        acc_ref[...] = alpha * acc_ref[...] + pv

    @pl.when(j == 0)
    def _():
        q = q_ref[...] * (DH ** -0.5)
        head = _lane(q.shape) >> 6
        qbd_ref[...] = jnp.concatenate(
            [jnp.where(head == h, q, 0.0) for h in range(H_FOX)], axis=0).astype(BF16)
        m_ref[...] = jnp.full(m_ref.shape, NEG_INF, F32)
        l_ref[...] = jnp.zeros_like(l_ref)
        acc_ref[...] = jnp.zeros_like(acc_ref)
        lfr = lfr_ref[...]
        fq_rows = -(_suffix_scan_rows(lfr) - lfr)
        fq_ref[...] = jnp.concatenate([fq_rows[:, h:h + 1] for h in range(H_FOX)], axis=0)
        lfn = lfn_ref[...]
        inc = _suffix_scan(lfn)
        carry_ref[...] = jnp.broadcast_to(inc[:, 0:1], carry_ref.shape)
        kn = _pad_rows(kn_ref[...], LANES).astype(BF16)
        vn = _pad_rows(vn_ref[...], LANES).astype(BF16)
        col = _lane((nrow, LANES))
        qrow = lax.broadcasted_iota(jnp.int32, (nrow, LANES), 0) & (ls - 1)
        update(_dot_nt(qbd_ref[...], kn), -(inc - lfn), col <= qrow, [vn], False)

    carry = carry_ref[...]
    fks = []
    for u in range(npg):
        x = lf_refs[u][...]
        inc = _suffix_scan(x)
        fks.append(-(carry + (inc - x)))
        carry = carry + inc[:, 0:1]
    carry_ref[...] = carry
    qbd = qbd_ref[...]
    s_raw = jnp.concatenate([_dot(qbd, k_refs[u][...].astype(BF16)) for u in range(npg)], axis=1)
    update(s_raw, jnp.concatenate(fks, axis=1), None,
           [v_refs[u][...].astype(BF16) for u in range(npg)], True)

    @pl.when(j == nj - 1)
    def _():
        acc = acc_ref[...] / l_ref[...]
        head = _lane((ls, D_FOX)) >> 6
        out = jnp.zeros((ls, D_FOX), F32)
        for h in range(H_FOX):
            out = out + jnp.where(head == h, acc[h * ls:(h + 1) * ls, :], 0.0)
        o_ref[...] = out


def fox_sample(z, k_new, v_new, lf_rows, lf_new_t, cache_lf_t, cache_kt, cache_vt, page_table, *, seq_len):
    t = z.shape[0]
    ls = seq_len
    nb = t // ls
    n_pages = page_table.shape[1]
    npg = _pick(n_pages, (16, 8, 4, 2))
    nj = n_pages // npg
    page = cache_kt.shape[2]
    nrow = H_FOX * ls

    def page_spec(u, rows):
        return pl.BlockSpec((None, rows, page),
                            lambda b, j, pt: (pt[b, n_pages - 1 - (j * npg + u)], 0, 0))

    row_spec = pl.BlockSpec((ls, D_FOX), lambda b, j, pt: (b, 0))
    return pl.pallas_call(
        functools.partial(_fox_s_kernel, npg=npg, nj=nj, ls=ls),
        grid_spec=pltpu.PrefetchScalarGridSpec(
            num_scalar_prefetch=1,
            grid=(nb, nj),
            in_specs=[
                row_spec, row_spec, row_spec,
                pl.BlockSpec((ls, LANES), lambda b, j, pt: (b, 0)),
                pl.BlockSpec((None, HEAD_PAD, LANES), lambda b, j, pt: (b, 0, 0)),
            ] + [page_spec(u, HEAD_PAD) for u in range(npg)]
              + [page_spec(u, D_FOX) for u in range(npg)] * 2,
            out_specs=row_spec,
            scratch_shapes=[
                pltpu.VMEM((nrow, D_FOX), BF16),
                pltpu.VMEM((nrow, 1), F32),
                pltpu.VMEM((HEAD_PAD, LANES), F32),
                pltpu.VMEM((nrow, 1), F32),
                pltpu.VMEM((nrow, 1), F32),
                pltpu.VMEM((nrow, D_FOX), F32),
            ]),
        out_shape=jax.ShapeDtypeStruct((t, D_FOX), F32),
        compiler_params=_params(("arbitrary", "arbitrary")),
        name="fox_sample",
    )(page_table, z, k_new, v_new, lf_rows, lf_new_t,
      *([cache_lf_t] * npg), *([cache_kt] * npg), *([cache_vt] * npg))


def _heads_to_sublanes(logf, nb, seq_len):
    x = logf[:, :H_FOX].reshape(nb, seq_len, H_FOX).transpose(0, 2, 1)
    return jnp.pad(x, ((0, 0), (0, HEAD_PAD - H_FOX), (0, 0)))


def _trunk(x, pos, mem_k, mem_v, ret_state, conv_state, past, p):
    nb, seq_len, d = x.shape
    t = nb * seq_len
    sample = past is not None
    cat_dtype = F32 if sample else BF16
    h = x.reshape(t, d)
    new_ret, new_conv = [], []
    k_sh = v_sh = logf_sh = None
    for layer in range(DEPTH):
        if layer < N_A:
            z = norm_matmul(h, p['g_mix_pre'][layer], p['w_in_a'], layer)
            s0 = (jnp.zeros((nb, H_RET, DK_RET, DV_RET), F32) if ret_state is None else ret_state[layer])
            r, s_new = retention(z, s0, pos, seq_len=seq_len, out_dtype=cat_dtype)
            new_ret.append(s_new)
            m = mem_attend(z, (2 * D_RET_QK + 2 * D_RET_V) // D_MEMQ, mem_k[layer], mem_v[layer],
                           seq_len=seq_len, out_dtype=cat_dtype)
            mixed, w_out, w_layer = r, p['w_out_a'], layer
        else:
            if layer == N_A:
                k_sh, v_sh, lf, *attn = shared_kv(h, p['g_kv'], p['wk'], p['wv'], p['wf'], p['bf'],
                                                  seq_len=seq_len, attn_copies=not sample)
                logf_sh = lf[:, :H_FOX]
                lf_t = _heads_to_sublanes(lf, nb, seq_len)
                if not sample:
                    kb, vbt = attn
                    f_t = forget_suffix(lf_t)
                    fa = forget_key_columns(f_t.transpose(0, 2, 1).reshape(t, HEAD_PAD))
                else:
                    cache_kt, cache_vt, cache_lf_t, page_table = past
                    lf_new_t = jnp.pad(lf_t, ((0, 0), (0, 0), (0, LANES - seq_len)))
            j = layer - N_A
            z = norm_matmul(h, p['g_mix_pre'][layer], p['w_in_b'], j)
            if not sample:
                o = fox_prompt(z, kb, vbt, f_t, fa, seq_len=seq_len)
            else:
                o = fox_sample(z, k_sh, v_sh, lf, lf_new_t, cache_lf_t, cache_kt, cache_vt, page_table,
                               seq_len=seq_len)
            m = mem_attend(z, D_FOX // D_MEMQ, mem_k[layer], mem_v[layer],
                           seq_len=seq_len, out_dtype=cat_dtype)
            mixed, w_out, w_layer = o, p['w_out_b'], j
        h, cs = mix_out_conv_ffn(mixed, m, w_out, w_layer, p['g_mix_post'][layer], h, layer, p,
                                 seq_len=seq_len,
                                 conv_state=None if conv_state is None else conv_state[layer])
        new_conv.append(cs)
    y = h.reshape(nb, seq_len, d)
    k_o = k_sh.reshape(nb, seq_len, H_FOX, DH)
    v_o = v_sh.reshape(nb, seq_len, H_FOX, DH)
    logf_o = logf_sh.reshape(nb, seq_len, H_FOX)
    return y, k_o, v_o, logf_o, jnp.stack(new_ret), jnp.stack(new_conv)


def kernel(x_prompt, x_sample, mem_prompt, cache_k, cache_v, cache_logf, page_table, cache_mem_k, cache_mem_v, state_ret, state_conv, g_mix_pre, g_mix_post, g_ffn_pre, g_ffn_post, w_in_a, w_out_a, w_in_b, w_out_b, w_mem_kv, g_kv, w_kv_shared, b_f, w_ffn_up, conv_w, conv_b, w_ffn_down):
    wf = jnp.pad(w_kv_shared[:, 2 * D_FOX:], ((0, 0), (0, LANES - H_FOX))).astype(BF16)
    bf = jnp.pad(b_f, (0, LANES - H_FOX)).reshape(1, LANES)
    p = {'g_mix_pre': g_mix_pre, 'g_mix_post': g_mix_post, 'g_ffn_pre': g_ffn_pre, 'g_ffn_post': g_ffn_post,
         'w_in_a': w_in_a.astype(BF16), 'w_out_a': w_out_a.astype(BF16),
         'w_in_b': w_in_b.astype(BF16), 'w_out_b': w_out_b.astype(BF16),
         'g_kv': g_kv, 'wk': w_kv_shared[:, :D_FOX].astype(BF16),
         'wv': w_kv_shared[:, D_FOX:2 * D_FOX].astype(BF16), 'wf': wf, 'bf': bf,
         'w_ffn_up': w_ffn_up.astype(BF16), 'conv_w': conv_w, 'conv_b': conv_b,
         'w_ffn_down': w_ffn_down.astype(BF16)}

    nb, seq_len, d = x_prompt.shape
    n_mem = mem_prompt.shape[1]
    mk, mv = mem_kv(mem_prompt.reshape(nb * n_mem, d), w_mem_kv.astype(BF16))
    mk = mk.reshape(DEPTH, nb, n_mem, D_MEMQ)
    mv = mv.reshape(DEPTH, nb, n_mem, D_MEMQ)
    pos_p = jnp.arange(seq_len, dtype=jnp.int32)
    y_p, k_p, v_p, logf_p, ret_p, conv_p = _trunk(x_prompt, pos_p, mk, mv, None, None, None, p)
    mem_k_p = mk.reshape(DEPTH, nb, n_mem, H_MEM, DH)
    mem_v_p = mv.reshape(DEPTH, nb, n_mem, H_MEM, DH)

    db, ls, _ = x_sample.shape
    n_phys, page = cache_k.shape[:2]
    past_len = page_table.shape[1] * page
    pos_s = past_len + jnp.arange(ls, dtype=jnp.int32)
    cache_lf_t = jnp.pad(cache_logf.transpose(0, 2, 1), ((0, 0), (0, HEAD_PAD - H_FOX), (0, 0)))
    cache_kt = cache_k.transpose(0, 2, 3, 1).reshape(n_phys, D_FOX, page)
    cache_vt = cache_v.transpose(0, 2, 3, 1).reshape(n_phys, D_FOX, page)
    past = (cache_kt, cache_vt, cache_lf_t, page_table)
    y_s, k_s, v_s, logf_s, ret_s, conv_s = _trunk(
        x_sample, pos_s, cache_mem_k.reshape(DEPTH, db, N_MEM, D_MEMQ),
        cache_mem_v.reshape(DEPTH, db, N_MEM, D_MEMQ), state_ret, state_conv, past, p)

    return (y_p, y_s, k_p, v_p, logf_p, k_s, v_s, logf_s,
            mem_k_p, mem_v_p, ret_p, ret_s, conv_p, conv_s)
```

```python
import functools

import jax
import jax.numpy as jnp
from jax import lax
from jax.experimental import pallas as pl
from jax.experimental.pallas import tpu as pltpu

F32 = jnp.float32
BF16 = jnp.bfloat16

D_MODEL = 1024
DEPTH = 4
N_A = 2
H_RET = 6
DK_RET = 64
DV_RET = 128
H_FOX = 12
DH = 64
H_MEM = 4
N_MEM = 256
D_RET_QK = H_RET * DK_RET
D_RET_V = H_RET * DV_RET
D_FOX = H_FOX * DH
D_MEMQ = H_MEM * DH
D_FF = 2816
RET_CHUNK = 128
ROPE_BASE = 10000.0
EPS = 1e-6

LANES = 128
HEAD_PAD = 16
VMEM_LIMIT = 48 * 1024 * 1024

NEG_INF = float("-inf")


def _params(sem):
    return pltpu.CompilerParams(dimension_semantics=sem, vmem_limit_bytes=VMEM_LIMIT)


def _rms(x, g):
    return x * lax.rsqrt(jnp.mean(x * x, axis=-1, keepdims=True) + EPS) * g


def _dot(a, b):
    return jnp.dot(a, b, preferred_element_type=F32)


def _dot_nt(a, b):
    return lax.dot_general(a, b, (((1,), (1,)), ((), ())), preferred_element_type=F32)


def _lane(shape):
    return lax.broadcasted_iota(jnp.int32, shape, len(shape) - 1)


def _pad_rows(x, n):
    if x.shape[0] == n:
        return x
    return jnp.concatenate([x, jnp.zeros((n - x.shape[0],) + x.shape[1:], x.dtype)], axis=0)


def _pick(n, cands):
    for c in cands:
        if n % c == 0:
            return c
    return n


NM_WEIGHT_BYTES = 6 * 1024 * 1024


def _nm_kernel(x_ref, g_ref, w_ref, o_ref, xn_ref, *, norm):
    @pl.when(pl.program_id(1) == 0)
    def _():
        x = x_ref[...]
        if norm:
            x = _rms(x, g_ref[...])
        xn_ref[...] = x.astype(BF16)

    o_ref[...] = _dot(xn_ref[...], w_ref[...])


def norm_matmul(x, g, w, layer, *, norm=True):
    t, d = x.shape
    n = w.shape[-1]
    tm = _pick(t, (512, 256))
    tn = n if d * n * 2 <= NM_WEIGHT_BYTES else _pick(n, (512, 256, 128))
    return pl.pallas_call(
        functools.partial(_nm_kernel, norm=norm),
        grid=(t // tm, n // tn),
        in_specs=[
            pl.BlockSpec((tm, d), lambda i, j: (i, 0)),
            pl.BlockSpec((1, d), lambda i, j: (0, 0)),
            pl.BlockSpec((None, d, tn), lambda i, j: (layer, 0, j)),
        ],
        out_specs=pl.BlockSpec((tm, tn), lambda i, j: (i, j)),
        out_shape=jax.ShapeDtypeStruct((t, n), F32),
        scratch_shapes=[pltpu.VMEM((tm, d), BF16)],
        compiler_params=_params(("arbitrary", "arbitrary")),
        name="norm_matmul",
    )(x, g.reshape(1, d), w)


def _memkv_kernel(x_ref, w_ref, k_ref, v_ref):
    acc = _dot(x_ref[...].astype(BF16), w_ref[...])
    k_ref[...] = acc[:, :D_MEMQ]
    v_ref[...] = acc[:, D_MEMQ:]


def mem_kv(x, w):
    t, d = x.shape
    tm = _pick(t, (1024, 512, 256))
    out = jax.ShapeDtypeStruct((DEPTH, t, D_MEMQ), F32)
    return pl.pallas_call(
        _memkv_kernel,
        grid=(t // tm, DEPTH),
        in_specs=[
            pl.BlockSpec((tm, d), lambda i, l: (i, 0)),
            pl.BlockSpec((None, d, 2 * D_MEMQ), lambda i, l: (l, 0, 0)),
        ],
        out_specs=[pl.BlockSpec((None, tm, D_MEMQ), lambda i, l: (l, i, 0))] * 2,
        out_shape=[out, out],
        compiler_params=_params(("arbitrary", "arbitrary")),
        name="mem_kv",
    )(x, w)


def _log_sigmoid(x):
    return jnp.minimum(x, 0.0) - jnp.log1p(jnp.exp(-jnp.abs(x)))


def _kv_kernel(x_ref, g_ref, wk_ref, wv_ref, wf_ref, bf_ref, k_ref, v_ref, lf_ref, *attn_refs):
    xn = _rms(x_ref[...], g_ref[...]).astype(BF16)
    k = _dot(xn, wk_ref[...])
    v = _dot(xn, wv_ref[...])
    k_ref[...] = k
    v_ref[...] = v
    lf_ref[...] = _log_sigmoid(_dot(xn, wf_ref[...]) + bf_ref[...])
    if attn_refs:
        kb_ref, vbt_ref = attn_refs
        for p in range(D_FOX // LANES):
            kb_ref[p] = k[:, p * LANES:(p + 1) * LANES].astype(BF16)
            vbt_ref[p] = v[:, p * LANES:(p + 1) * LANES].T.astype(BF16)


def shared_kv(h, g, wk, wv, wf, bf, *, seq_len, attn_copies):
    t, d = h.shape
    tm = _pick(t, (512, 256))
    npair = D_FOX // LANES
    full = lambda shape: pl.BlockSpec(shape, lambda i: (0,) * len(shape))
    out_specs = [
        pl.BlockSpec((tm, D_FOX), lambda i: (i, 0)),
        pl.BlockSpec((tm, D_FOX), lambda i: (i, 0)),
        pl.BlockSpec((tm, LANES), lambda i: (i, 0)),
    ]
    out_shape = [
        jax.ShapeDtypeStruct((t, D_FOX), F32),
        jax.ShapeDtypeStruct((t, D_FOX), F32),
        jax.ShapeDtypeStruct((t, LANES), F32),
    ]
    if attn_copies:
        tpb = seq_len // tm
        out_specs += [
            pl.BlockSpec((npair, tm, LANES), lambda i: (0, i, 0)),
            pl.BlockSpec((None, npair, LANES, tm), lambda i: (i // tpb, 0, 0, i % tpb)),
        ]
        out_shape += [
            jax.ShapeDtypeStruct((npair, t, LANES), BF16),
            jax.ShapeDtypeStruct((t // seq_len, npair, LANES, seq_len), BF16),
        ]
    return pl.pallas_call(
        _kv_kernel,
        grid=(t // tm,),
        in_specs=[
            pl.BlockSpec((tm, d), lambda i: (i, 0)),
            full((1, d)), full((d, D_FOX)), full((d, D_FOX)), full((d, LANES)), full((1, LANES)),
        ],
        out_specs=out_specs,
        out_shape=out_shape,
        compiler_params=_params(("arbitrary",)),
        name="shared_kv",
    )(h, g.reshape(1, d), wk, wv, wf, bf)


FFN_TILE = 512
FFN_ROWS = 16
FFN_CHUNK = 256
FFN_VMEM_LIMIT = 56 * 1024 * 1024


def _gelu_tanh(x):
    return 0.5 * x * (1.0 + jnp.tanh(0.7978845608028654 * (x + 0.044715 * (x * x * x))))


def _ffn_kernel(*refs, tm, tf, tiles_per_batch, seq_rows):
    (c1_ref, c2_ref, wo_ref, gmix_ref,
     h_ref, gpre_ref, wg_ref, wv_ref, cwg_ref, cwv_ref, cbg_ref, cbv_ref, wd_ref, gpost_ref) = refs[:14]
    if seq_rows is None:
        o_ref, tg_ref, tv_ref, hm_ref, xn_ref, ug_ref, uv_ref, hid_ref, cg_ref, cv_ref = refs[14:]
        p1g_ref = p1v_ref = p2g_ref = p2v_ref = None
    else:
        (p1g_ref, p1v_ref, p2g_ref, p2v_ref, o_ref, tg_ref, tv_ref,
         hm_ref, xn_ref, ug_ref, uv_ref, hid_ref) = refs[14:]
        cg_ref = cv_ref = None
    i = pl.program_id(0)
    rg = FFN_ROWS
    ch = FFN_CHUNK

    for u_ref, c_ref in ((ug_ref, cg_ref), (uv_ref, cv_ref)):
        if seq_rows is None:
            @pl.when((i % tiles_per_batch) == 0)
            def _():
                c_ref[...] = jnp.zeros((8, tf), F32)

            u_ref[0:8, :] = c_ref[...]
        else:
            u_ref[0:8, :] = jnp.zeros((8, tf), F32)

    if seq_rows is not None:
        r = lax.broadcasted_iota(jnp.int32, (rg, LANES), 0) & (seq_rows - 1)
        keep1 = r >= 1
        keep2 = r >= 2

    def pre(c):
        rows = slice(c * ch, (c + 1) * ch)
        n1 = c1_ref.shape[1]
        mix = (_dot(c1_ref[rows, :].astype(BF16), wo_ref[:n1, :])
               + _dot(c2_ref[rows, :].astype(BF16), wo_ref[n1:, :]))
        hm = h_ref[rows, :] + _rms(mix, gmix_ref[...])
        hm_ref[rows, :] = hm
        xn_ref[rows, :] = _rms(hm, gpre_ref[...]).astype(BF16)

    def up(c):
        pre(c)
        xc = xn_ref[c * ch:(c + 1) * ch, :]
        ug_ref[8 + c * ch:8 + (c + 1) * ch, :] = _dot(xc, wg_ref[...])
        uv_ref[8 + c * ch:8 + (c + 1) * ch, :] = _dot(xc, wv_ref[...])

    def conv(u_ref, cw_ref, cb_ref, p1_ref, p2_ref, r0, sl):
        a = u_ref[r0:r0 + rg + 8, sl]
        u = a[8:]
        u1 = pltpu.roll(a, 1, 0)[8:]
        u2 = pltpu.roll(a, 2, 0)[8:]
        if seq_rows is not None:
            u1 = jnp.where(keep1, u1, p1_ref[r0:r0 + rg, sl])
            u2 = jnp.where(keep2, u2, p2_ref[r0:r0 + rg, sl])
        return cb_ref[:, sl] + cw_ref[0:1, sl] * u2 + cw_ref[1:2, sl] * u1 + cw_ref[2:3, sl] * u

    def gate_mul(c):
        for lt in range(tf // LANES):
            sl = slice(lt * LANES, (lt + 1) * LANES)
            for r0 in range(c * ch, (c + 1) * ch, rg):
                gate = conv(ug_ref, cwg_ref, cbg_ref, p1g_ref, p2g_ref, r0, sl)
                val = conv(uv_ref, cwv_ref, cbv_ref, p1v_ref, p2v_ref, r0, sl)
                hid_ref[r0:r0 + rg, sl] = (_gelu_tanh(gate) * val).astype(BF16)

    def down(c):
        rows = slice(c * ch, (c + 1) * ch)
        o_ref[rows, :] = hm_ref[rows, :] + _rms(_dot(hid_ref[rows, :], wd_ref[...]), gpost_ref[...])

    nch = tm // ch
    ahead = 2
    for c in range(min(ahead, nch)):
        up(c)
    for c in range(nch):
        if c + ahead < nch:
            up(c + ahead)
        gate_mul(c)
        down(c)

    if seq_rows is None:
        cg_ref[...] = ug_ref[tm:tm + 8, :]
        cv_ref[...] = uv_ref[tm:tm + 8, :]
        tg_ref[0] = ug_ref[tm:tm + 8, :]
        tv_ref[0] = uv_ref[tm:tm + 8, :]
    else:
        tg_ref[...] = ug_ref[8:tm + 8, :]
        tv_ref[...] = uv_ref[8:tm + 8, :]


def mix_out_conv_ffn(c1, c2, w_out, w_layer, g_mix, h, layer, p, *, seq_len, conv_state=None):
    t, d = h.shape
    n1, n2 = c1.shape[1], c2.shape[1]
    nb = t // seq_len
    tf = D_FF
    w_up, cw, cb, w_down = p['w_ffn_up'], p['conv_w'], p['conv_b'], p['w_ffn_down']
    cb3 = cb.reshape(DEPTH, 1, 2 * D_FF)
    if conv_state is None:
        tm = _pick(seq_len, (FFN_TILE, FFN_CHUNK))
        tiles_per_batch = seq_len // tm
        seq_rows = None
        tail_shape = jax.ShapeDtypeStruct((t // tm, 8, D_FF), F32)
        tail_spec = pl.BlockSpec((1, 8, tf), lambda i: (i, 0, 0))
        extra_in, extra_specs = [], []
        extra_scratch = [pltpu.VMEM((8, tf), F32), pltpu.VMEM((8, tf), F32)]
    else:
        tm = _pick(t, (FFN_TILE, FFN_CHUNK))
        tiles_per_batch = 1
        seq_rows = seq_len
        tail_shape = jax.ShapeDtypeStruct((t, D_FF), F32)
        tail_spec = pl.BlockSpec((tm, tf), lambda i: (i, 0))
        st = conv_state
        row = jnp.arange(seq_len)[None, :, None]
        prev1 = jnp.where(row == 0, st[:, 1:2], 0.0).reshape(t, 2 * D_FF)
        prev2 = jnp.where(row == 0, st[:, 0:1], jnp.where(row == 1, st[:, 1:2], 0.0)).reshape(t, 2 * D_FF)
        extra_in = [prev1, prev1, prev2, prev2]
        gspec = pl.BlockSpec((tm, tf), lambda i: (i, 0))
        vspec = pl.BlockSpec((tm, tf), lambda i: (i, 1))
        extra_specs = [gspec, vspec, gspec, vspec]
        extra_scratch = []
    kern = functools.partial(_ffn_kernel, tm=tm, tf=tf, tiles_per_batch=tiles_per_batch, seq_rows=seq_rows)
    once = pl.Buffered(1)
    h_new, tail_g, tail_v = pl.pallas_call(
        kern,
        grid=(t // tm,),
        in_specs=[
            pl.BlockSpec((tm, n1), lambda i: (i, 0)),
            pl.BlockSpec((tm, n2), lambda i: (i, 0)),
            pl.BlockSpec((None, n1 + n2, d), lambda i: (w_layer, 0, 0), pipeline_mode=once),
            pl.BlockSpec((1, d), lambda i: (0, 0)),
            pl.BlockSpec((tm, d), lambda i: (i, 0)),
            pl.BlockSpec((1, d), lambda i: (0, 0)),
            pl.BlockSpec((None, d, tf), lambda i: (layer, 0, 0), pipeline_mode=once),
            pl.BlockSpec((None, d, tf), lambda i: (layer, 0, 1), pipeline_mode=once),
            pl.BlockSpec((None, 3, tf), lambda i: (layer, 0, 0)),
            pl.BlockSpec((None, 3, tf), lambda i: (layer, 0, 1)),
            pl.BlockSpec((None, 1, tf), lambda i: (layer, 0, 0)),
            pl.BlockSpec((None, 1, tf), lambda i: (layer, 0, 1)),
            pl.BlockSpec((None, tf, d), lambda i: (layer, 0, 0), pipeline_mode=once),
            pl.BlockSpec((1, d), lambda i: (0, 0)),
        ] + extra_specs,
        out_specs=[pl.BlockSpec((tm, d), lambda i: (i, 0)), tail_spec, tail_spec],
        out_shape=[jax.ShapeDtypeStruct((t, d), F32), tail_shape, tail_shape],
        scratch_shapes=[pltpu.VMEM((tm, d), F32), pltpu.VMEM((tm, d), BF16),
                        pltpu.VMEM((tm + 8, tf), F32), pltpu.VMEM((tm + 8, tf), F32),
                        pltpu.VMEM((tm, tf), BF16)] + extra_scratch,
        compiler_params=pltpu.CompilerParams(dimension_semantics=("arbitrary",),
                                             vmem_limit_bytes=FFN_VMEM_LIMIT),
        name="conv_ffn",
    )(c1, c2, w_out, g_mix.reshape(1, d), h, p['g_ffn_pre'][layer].reshape(1, d), w_up, w_up, cw, cw,
      cb3, cb3, w_down, p['g_ffn_post'][layer].reshape(1, d), *extra_in)
    if conv_state is None:
        last = slice(tiles_per_batch - 1, None, tiles_per_batch)
        new_state = jnp.concatenate([tail_g[last, 6:8], tail_v[last, 6:8]], axis=-1)
    else:
        new_state = jnp.concatenate([tail_g.reshape(nb, seq_len, D_FF)[:, seq_len - 2:],
                                     tail_v.reshape(nb, seq_len, D_FF)[:, seq_len - 2:]], axis=-1)
    return h_new, new_state


def _rotary(x, cos, sins):
    parts = []
    for p in range(x.shape[1] // LANES):
        xb = x[:, p * LANES:(p + 1) * LANES]
        first = (_lane(xb.shape) & (DK_RET // 2)) == 0
        sw = jnp.where(first, pltpu.roll(xb, LANES - DK_RET // 2, 1), pltpu.roll(xb, DK_RET // 2, 1))
        parts.append(xb * cos[:, p * LANES:(p + 1) * LANES] + sw * sins[:, p * LANES:(p + 1) * LANES])
    return jnp.concatenate(parts, axis=1)


def _ret_kernel(q_ref, k_ref, v_ref, g_ref, cos_ref, sin_ref, dm_ref, qd_ref, kd_ref, sd_ref, s0_ref,
                r_ref, so_ref, s_ref, *, tq, nq):
    c = RET_CHUNK
    qi = pl.program_id(1)
    zero_half = jnp.zeros((DK_RET, DV_RET), F32)

    @pl.when(qi == 0)
    def _():
        for h in range(H_RET):
            halves = [zero_half, zero_half]
            halves[h % 2] = s0_ref[h]
            s_ref[h] = jnp.concatenate(halves, axis=0)

    cos = cos_ref[...]
    sins = sin_ref[...]
    q = _rotary(q_ref[...], cos, sins)
    k = _rotary(k_ref[...], cos, sins) * (DK_RET ** -0.5)
    rows = max(tq, c)
    for cs in range(0, rows, c):
        n = min(c, tq)
        qc = _pad_rows(q[cs:cs + n], c)
        kc = _pad_rows(k[cs:cs + n], c)
        vc = _pad_rows(v_ref[cs:cs + n, :], c)
        qd = qc * qd_ref[...]
        kd = kc * kd_ref[...]
        for p in range(H_RET // 2):
            sl = slice(p * LANES, (p + 1) * LANES)
            q2, k2, qd2, kd2 = qc[:, sl], kc[:, sl], qd[:, sl], kd[:, sl]
            k2b = k2.astype(BF16)
            hi = _lane(q2.shape) >= DK_RET
            for e in range(2):
                h = 2 * p + e
                mine = hi if e else jnp.logical_not(hi)
                qm = jnp.where(mine, q2, 0.0).astype(BF16)
                qdm = jnp.where(mine, qd2, 0.0).astype(BF16)
                kdm = jnp.where(mine, kd2, 0.0)
                vh = vc[:, h * DV_RET:(h + 1) * DV_RET].astype(BF16)
                inner = (_dot_nt(qm, k2b) * dm_ref[h]).astype(BF16)
                s_old = s_ref[h]
                o = _dot(inner, vh) + _dot(qdm, s_old.astype(BF16))
                s_ref[h] = sd_ref[h] * s_old + _dot(kdm.T.astype(BF16), vh)
                cen = o - jnp.mean(o, axis=-1, keepdims=True)
                y = cen * lax.rsqrt(jnp.mean(cen * cen, axis=-1, keepdims=True) + EPS)
                gh = g_ref[cs:cs + n, h * DV_RET:(h + 1) * DV_RET]
                r = (gh / (1.0 + jnp.exp(-gh))) * y[:n]
                r_ref[cs:cs + n, h * DV_RET:(h + 1) * DV_RET] = r.astype(r_ref.dtype)

    @pl.when(qi == nq - 1)
    def _():
        for h in range(H_RET):
            e = h % 2
            so_ref[h] = s_ref[h][e * DK_RET:(e + 1) * DK_RET, :]


def _ret_tables(chunk):
    c = RET_CHUNK
    log_gamma = jnp.log1p(-(2.0 ** (-5.0 - jnp.arange(H_RET, dtype=F32))))
    idx = jnp.arange(c, dtype=F32)
    rel = idx[:, None] - idx[None, :]
    dmask = jnp.where(rel >= 0, jnp.exp(log_gamma[:, None, None] * jnp.maximum(rel, 0.0)), 0.0)
    qdec = jnp.exp(log_gamma[None, :] * (idx[:, None] + 1.0))
    kdec = jnp.exp(log_gamma[None, :] * (chunk - 1.0 - idx[:, None]))
    sdec = jnp.exp(log_gamma * chunk)
    qdec = jnp.repeat(qdec, DK_RET, axis=1)
    kdec = jnp.repeat(kdec, DK_RET, axis=1)
    sdec = jnp.broadcast_to(sdec[:, None, None], (H_RET, 1, DV_RET))
    return dmask, qdec, kdec, sdec


def _rope_tables(pos):
    half = DK_RET // 2
    inv = ROPE_BASE ** (-jnp.arange(half, dtype=F32) / half)
    ang = pos.astype(F32)[:, None] * inv[None, :]
    cos, sin = jnp.cos(ang), jnp.sin(ang)
    cos_t = jnp.tile(jnp.concatenate([cos, cos], axis=1), (1, H_RET))
    sin_t = jnp.tile(jnp.concatenate([-sin, sin], axis=1), (1, H_RET))
    return cos_t, sin_t


def retention(z, s0, pos, *, seq_len, out_dtype):
    t = z.shape[0]
    nb = t // seq_len
    chunk = RET_CHUNK if seq_len % RET_CHUNK == 0 else seq_len
    tq = _pick(seq_len, (512, 256, 128))
    nq = seq_len // tq
    dmask, qdec, kdec, sdec = _ret_tables(chunk)
    cos_t, sin_t = _rope_tables(pos)
    nqk = D_RET_QK
    full = lambda shape: pl.BlockSpec(shape, lambda b, i: (0,) * len(shape))
    return pl.pallas_call(
        functools.partial(_ret_kernel, tq=tq, nq=nq),
        grid=(nb, nq),
        in_specs=[
            pl.BlockSpec((tq, nqk), lambda b, i: (b * nq + i, 0)),
            pl.BlockSpec((tq, nqk), lambda b, i: (b * nq + i, 1)),
            pl.BlockSpec((tq, D_RET_V), lambda b, i: (b * nq + i, 1)),
            pl.BlockSpec((tq, D_RET_V), lambda b, i: (b * nq + i, 2)),
            pl.BlockSpec((tq, nqk), lambda b, i: (i, 0)),
            pl.BlockSpec((tq, nqk), lambda b, i: (i, 0)),
            full((H_RET, RET_CHUNK, RET_CHUNK)),
            full((RET_CHUNK, nqk)),
            full((RET_CHUNK, nqk)),
            full((H_RET, 1, DV_RET)),
            pl.BlockSpec((None, H_RET, DK_RET, DV_RET), lambda b, i: (b, 0, 0, 0)),
        ],
        out_specs=[
            pl.BlockSpec((tq, D_RET_V), lambda b, i: (b * nq + i, 0)),
            pl.BlockSpec((None, H_RET, DK_RET, DV_RET), lambda b, i: (b, 0, 0, 0)),
        ],
        out_shape=[jax.ShapeDtypeStruct((t, D_RET_V), out_dtype),
                   jax.ShapeDtypeStruct((nb, H_RET, DK_RET, DV_RET), F32)],
        scratch_shapes=[pltpu.VMEM((H_RET, 2 * DK_RET, DV_RET), F32)],
        compiler_params=_params(("arbitrary", "arbitrary")),
        name="retention",
    )(z, z, z, z, cos_t, sin_t, dmask, qdec, kdec, sdec, s0)


def _mem_kernel(q_ref, mk_ref, mv_ref, o_ref, *, tq):
    rows = max(tq, 16)
    q = _pad_rows(q_ref[...] * (DH ** -0.5), rows)
    for p in range(D_MEMQ // LANES):
        sl = slice(p * LANES, (p + 1) * LANES)
        q2 = q[:, sl]
        k2 = mk_ref[:, sl].astype(BF16)
        v2 = mv_ref[:, sl].astype(BF16)
        hi = _lane(q2.shape) >= DH
        outs = []
        for e in range(2):
            mine = hi if e else jnp.logical_not(hi)
            s = _dot_nt(jnp.where(mine, q2, 0.0).astype(BF16), k2)
            pe = jnp.exp(s - jnp.max(s, axis=-1, keepdims=True))
            outs.append(_dot(pe.astype(BF16), v2) / jnp.sum(pe, axis=-1, keepdims=True))
        o = jnp.where(hi, outs[1], outs[0])
        o_ref[:, sl] = o[:tq].astype(o_ref.dtype)


def mem_attend(z, col_block, mk, mv, *, seq_len, out_dtype):
    t = z.shape[0]
    nb = t // seq_len
    tq = _pick(seq_len, (512, 256, 128))
    nq = seq_len // tq
    return pl.pallas_call(
        functools.partial(_mem_kernel, tq=tq),
        grid=(nb, nq),
        in_specs=[
            pl.BlockSpec((tq, D_MEMQ), lambda b, i: (b * nq + i, col_block)),
            pl.BlockSpec((None, N_MEM, D_MEMQ), lambda b, i: (b, 0, 0)),
            pl.BlockSpec((None, N_MEM, D_MEMQ), lambda b, i: (b, 0, 0)),
        ],
        out_specs=pl.BlockSpec((tq, D_MEMQ), lambda b, i: (b * nq + i, 0)),
        out_shape=jax.ShapeDtypeStruct((t, D_MEMQ), out_dtype),
        compiler_params=_params(("arbitrary", "arbitrary")),
        name="mem_attend",
    )(z, mk, mv)


def _suffix_scan(x):
    lane = _lane(x.shape)
    t = x
    d = 1
    while d < LANES:
        t = t + jnp.where(lane + d < LANES, pltpu.roll(t, LANES - d, 1), 0.0)
        d *= 2
    return t


def _suffix_scan_rows(x):
    n = x.shape[0]
    row = lax.broadcasted_iota(jnp.int32, x.shape, 0)
    t = x
    d = 1
    while d < n:
        t = t + jnp.where(row + d < n, pltpu.roll(t, n - d, 0), 0.0)
        d *= 2
    return t


def _cumsum_kernel(x_ref, fo_ref):
    carry = jnp.zeros((HEAD_PAD, LANES), F32)
    for p in range(x_ref.shape[1] // LANES - 1, -1, -1):
        sl = slice(p * LANES, (p + 1) * LANES)
        x = x_ref[:, sl]
        inc = _suffix_scan(x)
        fo_ref[:, sl] = -(carry + (inc - x))
        carry = carry + inc[:, 0:1]


def forget_suffix(logf_t):
    nb, _, seq_len = logf_t.shape
    spec = pl.BlockSpec((None, HEAD_PAD, seq_len), lambda b: (b, 0, 0))
    return pl.pallas_call(
        _cumsum_kernel,
        grid=(nb,),
        in_specs=[spec],
        out_specs=spec,
        out_shape=jax.ShapeDtypeStruct((nb, HEAD_PAD, seq_len), F32),
        compiler_params=_params(("arbitrary",)),
        name="forget_suffix",
    )(logf_t)


def _split3(x):
    hi = x.astype(BF16).astype(F32)
    r = x - hi
    mid = r.astype(BF16).astype(F32)
    return hi, mid, r - mid


def _faug_kernel(f_ref, o_ref):
    f = f_ref[...]
    lane = _lane((f.shape[0], LANES))
    for p in range(H_FOX // 2):
        out = jnp.where((lane >= 2 * FSPLIT) & (lane < 3 * FSPLIT), 1.0, 0.0)
        for e in range(2):
            parts = _split3(f[:, 2 * p + e:2 * p + e + 1])
            for j in range(FSPLIT):
                out = jnp.where(lane == FSPLIT * e + j, parts[j], out)
        o_ref[p] = out.astype(BF16)


def forget_key_columns(f_col):
    t = f_col.shape[0]
    tm = _pick(t, (512, 256))
    npair = H_FOX // 2
    return pl.pallas_call(
        _faug_kernel,
        grid=(t // tm,),
        in_specs=[pl.BlockSpec((tm, HEAD_PAD), lambda i: (i, 0))],
        out_specs=pl.BlockSpec((npair, tm, LANES), lambda i: (0, i, 0)),
        out_shape=jax.ShapeDtypeStruct((npair, t, LANES), BF16),
        compiler_params=_params(("arbitrary",)),
        name="forget_key_columns",
    )(f_col)


SUBLANES = 8
FSPLIT = 3


def _fox_p_kernel(q_ref, kb_ref, vt_ref, fqr_ref, fa_ref, o_ref,
                  qt_ref, m_ref, l_ref, acc_ref, s4_ref, p2_ref, al_ref, *, tq, tk):
    qi = pl.program_id(1)
    ki = pl.program_id(2)
    npair = H_FOX // 2
    row_hi = lax.broadcasted_iota(jnp.int32, (LANES, tq), 0) >= DH
    nacc = 4

    @pl.when(ki == 0)
    def _():
        q = q_ref[...] * (DH ** -0.5)
        row = lax.broadcasted_iota(jnp.int32, (LANES, tq), 0)
        for p in range(npair):
            qt = q[:, p * LANES:(p + 1) * LANES].T
            for e in range(2):
                h = 2 * p + e
                qt_ref[h, 0:LANES, :] = jnp.where(row_hi == bool(e), qt, 0.0).astype(BF16)
                fq = _split3(fqr_ref[h:h + 1, :])
                ext = jnp.where((row >= FSPLIT * e) & (row < FSPLIT * (e + 1)), -1.0, 0.0)
                for j in range(FSPLIT):
                    ext = jnp.where(row == 2 * FSPLIT + j, fq[j], ext)
                qt_ref[h, LANES:2 * LANES, :] = ext.astype(BF16)
        m_ref[...] = jnp.full(m_ref.shape, NEG_INF, F32)
        l_ref[...] = jnp.zeros_like(l_ref)
        acc_ref[...] = jnp.zeros_like(acc_ref)

    def tree(op, xs):
        while len(xs) > 1:
            xs = [op(xs[i], xs[i + 1]) for i in range(0, len(xs) - 1, 2)] + ([xs[-1]] if len(xs) % 2 else [])
        return xs[0]

    def step(masked):
        def scores(p, slot):
            ka = jnp.concatenate([kb_ref[p], fa_ref[p]], axis=1)
            for e in range(2):
                s4_ref[slot + e] = _dot(ka, qt_ref[2 * p + e])

        scores(0, 0)

        def body(pp, carry):
            one_pair(2 * pp, 0)
            one_pair(2 * pp + 1, 2)
            return carry

        def one_pair(p, cur):
            vt = vt_ref[p]
            pvs = []
            scores(jnp.minimum(p + 1, npair - 1), 2 - cur)
            for e in range(2):
                h = 2 * p + e
                s_ref, p_ref = s4_ref.at[cur + e], p2_ref.at[e]
                m_all, l_all = m_ref[h], l_ref[h]
                m_news, l_news, alphas, nrows = [], [], [], []
                for c in range(tq // LANES):
                    cs = slice(c * LANES, (c + 1) * LANES)
                    nr = min(tk, LANES * (c + 1)) if masked else tk
                    nrows.append(nr)
                    mx = [jnp.full((SUBLANES, LANES), NEG_INF, F32)] * nacc
                    for r in range(nr // SUBLANES):
                        rs = slice(r * SUBLANES, (r + 1) * SUBLANES)
                        s = s_ref[rs, cs]
                        if masked and (r + 1) * SUBLANES > c * LANES:
                            key = lax.broadcasted_iota(jnp.int32, (SUBLANES, LANES), 0) + r * SUBLANES
                            qry = lax.broadcasted_iota(jnp.int32, (SUBLANES, LANES), 1) + c * LANES
                            s = jnp.where(key <= qry, s, NEG_INF)
                            s_ref[rs, cs] = s
                        mx[r % nacc] = jnp.maximum(mx[r % nacc], s)
                    m_prev = m_all[:, cs]
                    m_new = jnp.maximum(m_prev, jnp.max(tree(jnp.maximum, mx), axis=0, keepdims=True))
                    m_news.append(m_new)
                    alphas.append(jnp.exp(m_prev - m_new))
                for c in range(tq // LANES):
                    cs = slice(c * LANES, (c + 1) * LANES)
                    nr = nrows[c]
                    m16 = jnp.broadcast_to(m_news[c], (2 * SUBLANES, LANES))
                    sm = [jnp.zeros((2 * SUBLANES, LANES), F32)] * nacc
                    for r in range(nr // (2 * SUBLANES)):
                        rs = slice(r * 2 * SUBLANES, (r + 1) * 2 * SUBLANES)
                        pe = jnp.exp(s_ref[rs, cs] - m16)
                        sm[r % nacc] = sm[r % nacc] + pe
                        p_ref[rs, cs] = pe.astype(BF16)
                    if nr < tk:
                        p_ref[nr:tk, cs] = jnp.zeros((tk - nr, LANES), BF16)
                    l_news.append(alphas[c] * l_all[:, cs]
                                  + jnp.sum(tree(jnp.add, sm), axis=0, keepdims=True))
                m_ref[h] = jnp.concatenate(m_news, axis=1)
                l_ref[h] = jnp.concatenate(l_news, axis=1)
                al_ref[e] = jnp.concatenate(alphas, axis=1)
            pvs = [_dot(vt, p2_ref[e]) for e in range(2)]
            acc_ref[p] = (jnp.where(row_hi, al_ref[1], al_ref[0]) * acc_ref[p]
                          + jnp.where(row_hi, pvs[1], pvs[0]))

        lax.fori_loop(0, npair // 2, body, 0)

    @pl.when(ki < qi)
    def _():
        step(False)

    @pl.when(ki == qi)
    def _():
        step(True)
        for p in range(npair):
            denom = jnp.where(row_hi, l_ref[2 * p + 1], l_ref[2 * p])
            o_ref[:, p * LANES:(p + 1) * LANES] = (acc_ref[p] / denom).T.astype(o_ref.dtype)


def fox_prompt(z, kb, vbt, f_t, fa, *, seq_len):
    t = z.shape[0]
    nb = t // seq_len
    tq = tk = _pick(seq_len, (512, 256, 128))
    nq = seq_len // tq
    npair = H_FOX // 2
    return pl.pallas_call(
        functools.partial(_fox_p_kernel, tq=tq, tk=tk),
        grid=(nb, nq, nq),
        in_specs=[
            pl.BlockSpec((tq, D_FOX), lambda b, i, j: (b * nq + i, 0)),
            pl.BlockSpec((npair, tk, LANES), lambda b, i, j: (0, b * nq + jnp.minimum(i, j), 0)),
            pl.BlockSpec((None, npair, LANES, tk), lambda b, i, j: (b, 0, 0, jnp.minimum(i, j))),
            pl.BlockSpec((None, HEAD_PAD, tq), lambda b, i, j: (b, 0, i)),
            pl.BlockSpec((npair, tk, LANES), lambda b, i, j: (0, b * nq + jnp.minimum(i, j), 0)),
        ],
        out_specs=pl.BlockSpec((tq, D_FOX), lambda b, i, j: (b * nq + i, 0)),
        out_shape=jax.ShapeDtypeStruct((t, D_FOX), BF16),
        scratch_shapes=[
            pltpu.VMEM((H_FOX, 2 * LANES, tq), BF16),
            pltpu.VMEM((H_FOX, 1, tq), F32),
            pltpu.VMEM((H_FOX, 1, tq), F32),
            pltpu.VMEM((npair, LANES, tq), F32),
            pltpu.VMEM((4, tk, tq), F32),
            pltpu.VMEM((2, tk, tq), BF16),
            pltpu.VMEM((2, 1, tq), F32),
        ],
        compiler_params=_params(("arbitrary", "arbitrary", "arbitrary")),
        name="fox_prompt",
    )(z, kb, vbt, f_t, fa)


def _fox_s_kernel(pt_ref, q_ref, kn_ref, vn_ref, lfr_ref, lfn_ref, *rest, npg, nj, ls):
    lf_refs = rest[:npg]
    k_refs = rest[npg:2 * npg]
    v_refs = rest[2 * npg:3 * npg]
    o_ref, qbd_ref, fq_ref, carry_ref, m_ref, l_ref, acc_ref = rest[3 * npg:]
    j = pl.program_id(1)
    nrow = H_FOX * ls

    def update(s_raw, fk, mask, vs, v_transposed):
        fq = fq_ref[...]
        s = jnp.concatenate(
            [s_raw[h * ls:(h + 1) * ls, :] + fq[h * ls:(h + 1) * ls, :] - fk[h:h + 1, :]
             for h in range(H_FOX)], axis=0)
        if mask is not None:
            s = jnp.where(mask, s, NEG_INF)
        m_prev = m_ref[...]
        m_new = jnp.maximum(m_prev, jnp.max(s, axis=-1, keepdims=True))
        alpha = jnp.exp(m_prev - m_new)
        pe = jnp.exp(s - m_new)
        l_ref[...] = alpha * l_ref[...] + jnp.sum(pe, axis=-1, keepdims=True)
        m_ref[...] = m_new
        pb = pe.astype(BF16)
        mm = _dot_nt if v_transposed else _dot
        pv = mm(pb[:, :LANES], vs[0])
        for u in range(1, len(vs)):
            pv = pv + mm(pb[:, u * LANES:(u + 1) * LANES], vs[u])
        acc_ref[...] = alpha * acc_ref[...] + pv

    @pl.when(j == 0)
    def _():
        q = q_ref[...] * (DH ** -0.5)
        head = _lane(q.shape) >> 6
        qbd_ref[...] = jnp.concatenate(
            [jnp.where(head == h, q, 0.0) for h in range(H_FOX)], axis=0).astype(BF16)
        m_ref[...] = jnp.full(m_ref.shape, NEG_INF, F32)
        l_ref[...] = jnp.zeros_like(l_ref)
        acc_ref[...] = jnp.zeros_like(acc_ref)
        lfr = lfr_ref[...]
        fq_rows = -(_suffix_scan_rows(lfr) - lfr)
        fq_ref[...] = jnp.concatenate([fq_rows[:, h:h + 1] for h in range(H_FOX)], axis=0)
        lfn = lfn_ref[...]
        inc = _suffix_scan(lfn)
        carry_ref[...] = jnp.broadcast_to(inc[:, 0:1], carry_ref.shape)
        kn = _pad_rows(kn_ref[...], LANES).astype(BF16)
        vn = _pad_rows(vn_ref[...], LANES).astype(BF16)
        col = _lane((nrow, LANES))
        qrow = lax.broadcasted_iota(jnp.int32, (nrow, LANES), 0) & (ls - 1)
        update(_dot_nt(qbd_ref[...], kn), -(inc - lfn), col <= qrow, [vn], False)

    carry = carry_ref[...]
    fks = []
    for u in range(npg):
        x = lf_refs[u][...]
        inc = _suffix_scan(x)
        fks.append(-(carry + (inc - x)))
        carry = carry + inc[:, 0:1]
    carry_ref[...] = carry
    qbd = qbd_ref[...]
    s_raw = jnp.concatenate([_dot(qbd, k_refs[u][...].astype(BF16)) for u in range(npg)], axis=1)
    update(s_raw, jnp.concatenate(fks, axis=1), None,
           [v_refs[u][...].astype(BF16) for u in range(npg)], True)

    @pl.when(j == nj - 1)
    def _():
        acc = acc_ref[...] / l_ref[...]
        head = _lane((ls, D_FOX)) >> 6
        out = jnp.zeros((ls, D_FOX), F32)
        for h in range(H_FOX):
            out = out + jnp.where(head == h, acc[h * ls:(h + 1) * ls, :], 0.0)
        o_ref[...] = out


def fox_sample(z, k_new, v_new, lf_rows, lf_new_t, cache_lf_t, cache_kt, cache_vt, page_table, *, seq_len):
    t = z.shape[0]
    ls = seq_len
    nb = t // ls
    n_pages = page_table.shape[1]
    npg = _pick(n_pages, (16, 8, 4, 2))
    nj = n_pages // npg
    page = cache_kt.shape[2]
    nrow = H_FOX * ls

    def page_spec(u, rows):
        return pl.BlockSpec((None, rows, page),
                            lambda b, j, pt: (pt[b, n_pages - 1 - (j * npg + u)], 0, 0))

    row_spec = pl.BlockSpec((ls, D_FOX), lambda b, j, pt: (b, 0))
    return pl.pallas_call(
        functools.partial(_fox_s_kernel, npg=npg, nj=nj, ls=ls),
        grid_spec=pltpu.PrefetchScalarGridSpec(
            num_scalar_prefetch=1,
            grid=(nb, nj),
            in_specs=[
                row_spec, row_spec, row_spec,
                pl.BlockSpec((ls, LANES), lambda b, j, pt: (b, 0)),
                pl.BlockSpec((None, HEAD_PAD, LANES), lambda b, j, pt: (b, 0, 0)),
            ] + [page_spec(u, HEAD_PAD) for u in range(npg)]
              + [page_spec(u, D_FOX) for u in range(npg)] * 2,
            out_specs=row_spec,
            scratch_shapes=[
                pltpu.VMEM((nrow, D_FOX), BF16),
                pltpu.VMEM((nrow, 1), F32),
                pltpu.VMEM((HEAD_PAD, LANES), F32),
                pltpu.VMEM((nrow, 1), F32),
                pltpu.VMEM((nrow, 1), F32),
                pltpu.VMEM((nrow, D_FOX), F32),
            ]),
        out_shape=jax.ShapeDtypeStruct((t, D_FOX), F32),
        compiler_params=_params(("arbitrary", "arbitrary")),
        name="fox_sample",
    )(page_table, z, k_new, v_new, lf_rows, lf_new_t,
      *([cache_lf_t] * npg), *([cache_kt] * npg), *([cache_vt] * npg))


def _heads_to_sublanes(logf, nb, seq_len):
    x = logf[:, :H_FOX].reshape(nb, seq_len, H_FOX).transpose(0, 2, 1)
    return jnp.pad(x, ((0, 0), (0, HEAD_PAD - H_FOX), (0, 0)))


def _trunk(x, pos, mem_k, mem_v, ret_state, conv_state, past, p):
    nb, seq_len, d = x.shape
    t = nb * seq_len
    sample = past is not None
    cat_dtype = F32 if sample else BF16
    h = x.reshape(t, d)
    new_ret, new_conv = [], []
    k_sh = v_sh = logf_sh = None
    for layer in range(DEPTH):
        if layer < N_A:
            z = norm_matmul(h, p['g_mix_pre'][layer], p['w_in_a'], layer)
            s0 = (jnp.zeros((nb, H_RET, DK_RET, DV_RET), F32) if ret_state is None else ret_state[layer])
            r, s_new = retention(z, s0, pos, seq_len=seq_len, out_dtype=cat_dtype)
            new_ret.append(s_new)
            m = mem_attend(z, (2 * D_RET_QK + 2 * D_RET_V) // D_MEMQ, mem_k[layer], mem_v[layer],
                           seq_len=seq_len, out_dtype=cat_dtype)
            mixed, w_out, w_layer = r, p['w_out_a'], layer
        else:
            if layer == N_A:
                k_sh, v_sh, lf, *attn = shared_kv(h, p['g_kv'], p['wk'], p['wv'], p['wf'], p['bf'],
                                                  seq_len=seq_len, attn_copies=not sample)
                logf_sh = lf[:, :H_FOX]
                lf_t = _heads_to_sublanes(lf, nb, seq_len)
                if not sample:
                    kb, vbt = attn
                    f_t = forget_suffix(lf_t)
                    fa = forget_key_columns(f_t.transpose(0, 2, 1).reshape(t, HEAD_PAD))
                else:
                    cache_kt, cache_vt, cache_lf_t, page_table = past
                    lf_new_t = jnp.pad(lf_t, ((0, 0), (0, 0), (0, LANES - seq_len)))
            j = layer - N_A
            z = norm_matmul(h, p['g_mix_pre'][layer], p['w_in_b'], j)
            if not sample:
                o = fox_prompt(z, kb, vbt, f_t, fa, seq_len=seq_len)
            else:
                o = fox_sample(z, k_sh, v_sh, lf, lf_new_t, cache_lf_t, cache_kt, cache_vt, page_table,
                               seq_len=seq_len)
            m = mem_attend(z, D_FOX // D_MEMQ, mem_k[layer], mem_v[layer],
                           seq_len=seq_len, out_dtype=cat_dtype)
            mixed, w_out, w_layer = o, p['w_out_b'], j
        h, cs = mix_out_conv_ffn(mixed, m, w_out, w_layer, p['g_mix_post'][layer], h, layer, p,
                                 seq_len=seq_len,
                                 conv_state=None if conv_state is None else conv_state[layer])
        new_conv.append(cs)
    y = h.reshape(nb, seq_len, d)
    k_o = k_sh.reshape(nb, seq_len, H_FOX, DH)
    v_o = v_sh.reshape(nb, seq_len, H_FOX, DH)
    logf_o = logf_sh.reshape(nb, seq_len, H_FOX)
    return y, k_o, v_o, logf_o, jnp.stack(new_ret), jnp.stack(new_conv)


def kernel(x_prompt, x_sample, mem_prompt, cache_k, cache_v, cache_logf, page_table, cache_mem_k, cache_mem_v, state_ret, state_conv, g_mix_pre, g_mix_post, g_ffn_pre, g_ffn_post, w_in_a, w_out_a, w_in_b, w_out_b, w_mem_kv, g_kv, w_kv_shared, b_f, w_ffn_up, conv_w, conv_b, w_ffn_down):
    wf = jnp.pad(w_kv_shared[:, 2 * D_FOX:], ((0, 0), (0, LANES - H_FOX))).astype(BF16)
    bf = jnp.pad(b_f, (0, LANES - H_FOX)).reshape(1, LANES)
    p = {'g_mix_pre': g_mix_pre, 'g_mix_post': g_mix_post, 'g_ffn_pre': g_ffn_pre, 'g_ffn_post': g_ffn_post,
         'w_in_a': w_in_a.astype(BF16), 'w_out_a': w_out_a.astype(BF16),
         'w_in_b': w_in_b.astype(BF16), 'w_out_b': w_out_b.astype(BF16),
         'g_kv': g_kv, 'wk': w_kv_shared[:, :D_FOX].astype(BF16),
         'wv': w_kv_shared[:, D_FOX:2 * D_FOX].astype(BF16), 'wf': wf, 'bf': bf,
         'w_ffn_up': w_ffn_up.astype(BF16), 'conv_w': conv_w, 'conv_b': conv_b,
         'w_ffn_down': w_ffn_down.astype(BF16)}

    nb, seq_len, d = x_prompt.shape
    n_mem = mem_prompt.shape[1]
    mk, mv = mem_kv(mem_prompt.reshape(nb * n_mem, d), w_mem_kv.astype(BF16))
    mk = mk.reshape(DEPTH, nb, n_mem, D_MEMQ)
    mv = mv.reshape(DEPTH, nb, n_mem, D_MEMQ)
    pos_p = jnp.arange(seq_len, dtype=jnp.int32)
    y_p, k_p, v_p, logf_p, ret_p, conv_p = _trunk(x_prompt, pos_p, mk, mv, None, None, None, p)
    mem_k_p = mk.reshape(DEPTH, nb, n_mem, H_MEM, DH)
    mem_v_p = mv.reshape(DEPTH, nb, n_mem, H_MEM, DH)

    db, ls, _ = x_sample.shape
    n_phys, page = cache_k.shape[:2]
    past_len = page_table.shape[1] * page
    pos_s = past_len + jnp.arange(ls, dtype=jnp.int32)
    cache_lf_t = jnp.pad(cache_logf.transpose(0, 2, 1), ((0, 0), (0, HEAD_PAD - H_FOX), (0, 0)))
    cache_kt = cache_k.transpose(0, 2, 3, 1).reshape(n_phys, D_FOX, page)
    cache_vt = cache_v.transpose(0, 2, 3, 1).reshape(n_phys, D_FOX, page)
    past = (cache_kt, cache_vt, cache_lf_t, page_table)
    y_s, k_s, v_s, logf_s, ret_s, conv_s = _trunk(
        x_sample, pos_s, cache_mem_k.reshape(DEPTH, db, N_MEM, D_MEMQ),
        cache_mem_v.reshape(DEPTH, db, N_MEM, D_MEMQ), state_ret, state_conv, past, p)

    return (y_p, y_s, k_p, v_p, logf_p, k_s, v_s, logf_s,
            mem_k_p, mem_v_p, ret_p, ret_s, conv_p, conv_s)
```
